```python
import jax, jax.numpy as jnp
from jax import lax
import numpy as np


D_MODEL = 4096
BATCH = 16
SEQ = 256
DEPTH = 4
DEC_BATCH = 8
DEC_SEQ = 1024
PAST_LEN = 256

GRID_W = 64
HEAD_DIM = 128
NA_HEADS = 16
SWA_HEADS = 16
SWA_KV_HEADS = 4
SWA_GROUP = SWA_HEADS // SWA_KV_HEADS
MIX_A = NA_HEADS * HEAD_DIM
MIX_B = SWA_HEADS * HEAD_DIM
MIX_WIDTH = MIX_A + MIX_B
KV_B = SWA_KV_HEADS * HEAD_DIM
IN_SPLITS = (MIX_A, MIX_A, MIX_A, MIX_B, KV_B, KV_B)
IN_WIDTH = 3 * MIX_A + MIX_B + 2 * KV_B
NA_ROWS = 8
NA_COLS = 16
SWA_WINDOW = 128
SWA_BLOCK = 128
CTX_BLOCK = 128
ROPE_BASE = 10000.0
ATTN_SCALE = HEAD_DIM ** -0.5
PEER_HEADS = 8
PEER_KEY_DIM = 128
N_KEYS = 128
N_EXPERTS = N_KEYS * N_KEYS
PEER_TOPK = 16
PEER_BLOCK = 64
EPS = 1e-6
NEG_INF = -1e30

kernel_name = 'hybrid_na_swa_peer_diffusion_step'


def rmsnorm(x, g):
    x32 = x.astype(jnp.float32)
    y = x32 * lax.rsqrt(jnp.mean(x32 * x32, axis=-1, keepdims=True) + EPS)
    return (y * g.astype(jnp.float32)).astype(x.dtype)


def adaln(cond, w_mod, b_mod):
    return jnp.split(jax.nn.silu(cond) @ w_mod + b_mod, 6, axis=-1)


def modulate(h, shift, scale):
    return h * (1 + scale) + shift


def joint_softmax(parts):
    sizes = [p.shape[-1] for p in parts]
    s = jnp.concatenate([p.astype(jnp.float32) for p in parts], axis=-1)
    p = jax.nn.softmax(s, axis=-1)
    return jnp.split(p, np.cumsum(sizes)[:-1].tolist(), axis=-1)


def split_proj(p):
    B, L, _ = p.shape
    qa, ka, va, qb, kb, vb = jnp.split(p, np.cumsum(IN_SPLITS)[:-1].tolist(), axis=-1)
    return (qa.reshape(B, L, NA_HEADS, HEAD_DIM), ka.reshape(B, L, NA_HEADS, HEAD_DIM),
            va.reshape(B, L, NA_HEADS, HEAD_DIM),
            qb.reshape(B, L, SWA_KV_HEADS, SWA_GROUP, HEAD_DIM),
            kb.reshape(B, L, SWA_KV_HEADS, HEAD_DIM), vb.reshape(B, L, SWA_KV_HEADS, HEAD_DIM))


def _rot(xh, ang):
    a, b = jnp.split(xh, 2, axis=-1)
    shp = (1, ang.shape[0]) + (1,) * (xh.ndim - 3) + (ang.shape[1],)
    cos = jnp.cos(ang).reshape(shp)
    sin = jnp.sin(ang).reshape(shp)
    return jnp.concatenate([a * cos - b * sin, a * sin + b * cos], axis=-1)


def axial_rope(x):
    T = x.shape[1]
    t = jnp.arange(T)
    row = (t // GRID_W).astype(jnp.float32)
    col = (t % GRID_W).astype(jnp.float32)
    nf = HEAD_DIM // 4
    inv = ROPE_BASE ** (-jnp.arange(nf, dtype=jnp.float32) / nf)
    x32 = x.astype(jnp.float32)
    xr, xc = jnp.split(x32, 2, axis=-1)
    out = jnp.concatenate([_rot(xr, row[:, None] * inv), _rot(xc, col[:, None] * inv)], axis=-1)
    return out.astype(x.dtype)


def ctx_self_attention(q, k, v, sink):
    B, L = q.shape[:2]
    nqb = L // CTX_BLOCK
    qb = q.reshape((B, nqb, CTX_BLOCK) + q.shape[2:]).swapaxes(0, 1)

    def block(qblk):
        s = jnp.einsum('bqkgd,bskd->bkgqs', qblk, k) * ATTN_SCALE
        parts = [s]
        if sink is not None:
            parts.append(jnp.broadcast_to(sink[None, :, :, None, None].astype(jnp.float32), s.shape[:-1] + (1,)))
        p = joint_softmax(parts)[0]
        return jnp.einsum('bkgqs,bskd->bqkgd', p.astype(v.dtype), v)

    o = lax.map(block, qb)
    return o.swapaxes(0, 1).reshape(B, L, -1)


def na_latent(q, k, v, k_ctx, v_ctx, rpb):
    B, T, H, Dh = q.shape
    rows = T // GRID_W
    kr = min(NA_ROWS, rows)
    r = np.arange(rows)
    rs = np.clip(r - kr // 2, 0, rows - kr)
    key_rows = rs[:, None] + np.arange(kr)[None, :]
    dr = key_rows - r[:, None] + (NA_ROWS - 1)
    cq = np.arange(GRID_W)
    cs = np.clip(cq - NA_COLS // 2, 0, GRID_W - NA_COLS)
    col_ok = (cq[None, :] >= cs[:, None]) & (cq[None, :] < cs[:, None] + NA_COLS)
    dc = np.clip(cq[None, :] - cq[:, None] + (NA_COLS - 1), 0, 2 * NA_COLS - 2)
    bias = rpb[:, dr[:, None, :, None], dc[None, :, None, :]]
    bias = bias.reshape(H, rows, GRID_W, kr * GRID_W).astype(jnp.float32)
    mask = np.broadcast_to(col_ok[:, None, :], (GRID_W, kr, GRID_W)).reshape(GRID_W, kr * GRID_W)
    qg = q.reshape(B, rows, GRID_W, H, Dh)
    kg = k.reshape(B, rows, GRID_W, H, Dh)[:, key_rows].reshape(B, rows, kr * GRID_W, H, Dh)
    vg = v.reshape(B, rows, GRID_W, H, Dh)[:, key_rows].reshape(B, rows, kr * GRID_W, H, Dh)
    s_lat = jnp.einsum('brqhd,brkhd->bhrqk', qg, kg).astype(jnp.float32) * ATTN_SCALE + bias
    s_lat = jnp.where(mask, s_lat, NEG_INF)
    s_ctx = jnp.einsum('brqhd,bchd->bhrqc', qg, k_ctx) * ATTN_SCALE
    p_lat, p_ctx = joint_softmax([s_lat, s_ctx])
    o = (jnp.einsum('bhrqk,brkhd->brqhd', p_lat.astype(v.dtype), vg)
         + jnp.einsum('bhrqc,bchd->brqhd', p_ctx.astype(v.dtype), v_ctx))
    return o.reshape(B, T, H * Dh)


def swa_latent(q, k, v, k_ctx, v_ctx, sink):
    B, T, KV, G, Dh = q.shape
    nb = T // SWA_BLOCK
    pad = ((0, 0), (SWA_BLOCK, SWA_BLOCK), (0, 0), (0, 0))
    kp = jnp.pad(k, pad).reshape(B, nb + 2, SWA_BLOCK, KV, Dh)
    vp = jnp.pad(v, pad).reshape(B, nb + 2, SWA_BLOCK, KV, Dh)
    kb = jnp.concatenate([kp[:, :-2], kp[:, 1:-1], kp[:, 2:]], axis=2)
    vb = jnp.concatenate([vp[:, :-2], vp[:, 1:-1], vp[:, 2:]], axis=2)
    qb = q.reshape(B, nb, SWA_BLOCK, KV, G, Dh)
    j = np.arange(nb)[:, None, None]
    qi = np.arange(SWA_BLOCK)[None, :, None]
    si = np.arange(3 * SWA_BLOCK)[None, None, :]
    kpos = (j - 1) * SWA_BLOCK + si
    qpos = j * SWA_BLOCK + qi
    valid = (np.abs(kpos - qpos) <= SWA_WINDOW) & (kpos >= 0) & (kpos < T)
    s_lat = jnp.einsum('bnqkgd,bnskd->bkgnqs', qb, kb).astype(jnp.float32) * ATTN_SCALE
    s_lat = jnp.where(valid, s_lat, NEG_INF)
    s_ctx = jnp.einsum('bnqkgd,bckd->bkgnqc', qb, k_ctx) * ATTN_SCALE
    s_sink = jnp.broadcast_to(sink[None, :, :, None, None, None].astype(jnp.float32), s_lat.shape[:-1] + (1,))
    p_lat, p_ctx, _ = joint_softmax([s_lat, s_ctx, s_sink])
    o = (jnp.einsum('bkgnqs,bnskd->bnqkgd', p_lat.astype(v.dtype), vb)
         + jnp.einsum('bkgnqc,bckd->bnqkgd', p_ctx.astype(v.dtype), v_ctx))
    return o.reshape(B, T, KV * G * Dh)


def merge_groups(oa, ob, ga, gb, w_out):
    return jnp.concatenate([rmsnorm(oa, ga), rmsnorm(ob, gb)], axis=-1) @ w_out


def peer(h, wq, subkeys, u, v):
    B, L, D = h.shape
    T = B * L
    x = h.reshape(T, D)
    q = (x @ wq).reshape(T, PEER_HEADS, 2, PEER_KEY_DIM)
    s = jnp.einsum('thpd,hpkd->thpk', q, subkeys).astype(jnp.float32)
    sv, si = lax.top_k(s, PEER_TOPK)
    cand = sv[:, :, 0, :, None] + sv[:, :, 1, None, :]
    cv, cp = lax.top_k(cand.reshape(T, PEER_HEADS, PEER_TOPK * PEER_TOPK), PEER_TOPK)
    e1 = jnp.take_along_axis(si[:, :, 0], cp // PEER_TOPK, axis=-1)
    e2 = jnp.take_along_axis(si[:, :, 1], cp % PEER_TOPK, axis=-1)
    idx = (e1 * N_KEYS + e2).reshape(T, PEER_HEADS * PEER_TOPK)
    g = jax.nn.softmax(cv, axis=-1).reshape(T, PEER_HEADS * PEER_TOPK).astype(h.dtype)
    nblk = T // PEER_BLOCK

    def block(args):
        xb, ib, gb = args
        ub = jnp.take(u, ib, axis=0)
        a = gb * jax.nn.gelu(jnp.einsum('tkd,td->tk', ub, xb), approximate=False)
        vb = jnp.take(v, ib, axis=0)
        return jnp.einsum('tk,tkd->td', a, vb)

    out = lax.map(block, (x.reshape(nblk, PEER_BLOCK, D), idx.reshape(nblk, PEER_BLOCK, -1),
                          g.reshape(nblk, PEER_BLOCK, -1)))
    return out.reshape(B, L, D)


def context_layer(x, mod, w_in, sink, ga, gb, w_out, g1, g2, wq, subkeys, u, v):
    shift1, scale1, gate1, shift2, scale2, gate2 = mod
    h = modulate(rmsnorm(x, g1), shift1, scale1)
    qa, ka, va, qb, kb, vb = split_proj(h @ w_in)
    oa = ctx_self_attention(qa[:, :, :, None, :], ka, va, None)
    ob = ctx_self_attention(qb, kb, vb, sink.reshape(SWA_KV_HEADS, SWA_GROUP))
    x = x + gate1 * merge_groups(oa, ob, ga, gb, w_out)
    x = x + gate2 * peer(modulate(rmsnorm(x, g2), shift2, scale2), wq, subkeys, u, v)
    return x, ka, va, kb, vb


def latent_layer(x, mod, k_ctx_a, v_ctx_a, k_ctx_b, v_ctx_b, w_in, rpb, sink, ga, gb, w_out, g1, g2,
                 wq, subkeys, u, v):
    shift1, scale1, gate1, shift2, scale2, gate2 = mod
    h = modulate(rmsnorm(x, g1), shift1, scale1)
    qa, ka, va, qb, kb, vb = split_proj(h @ w_in)
    oa = na_latent(qa, ka, va, k_ctx_a, v_ctx_a, rpb)
    ob = swa_latent(axial_rope(qb), axial_rope(kb), vb, k_ctx_b, v_ctx_b,
                    sink.reshape(SWA_KV_HEADS, SWA_GROUP))
    x = x + gate1 * merge_groups(oa, ob, ga, gb, w_out)
    x = x + gate2 * peer(modulate(rmsnorm(x, g2), shift2, scale2), wq, subkeys, u, v)
    return x


def setup_inputs(seed: int = 0) -> dict:
    key = jax.random.key(seed)
    ks = jax.random.split(key, 24)
    nrm = lambda k, shp, s: jax.random.normal(k, shp, jnp.float32) * s
    return {
        'x_prompt': nrm(ks[0], (BATCH, SEQ, D_MODEL), 1.0),
        'x_sample': nrm(ks[1], (DEC_BATCH, DEC_SEQ, D_MODEL), 1.0),
        'cache_na_k': nrm(ks[2], (DEC_BATCH, DEPTH, PAST_LEN, NA_HEADS, HEAD_DIM), 1.0),
        'cache_na_v': nrm(ks[3], (DEC_BATCH, DEPTH, PAST_LEN, NA_HEADS, HEAD_DIM), 1.0),
        'cache_swa_k': nrm(ks[4], (DEC_BATCH, DEPTH, PAST_LEN, SWA_KV_HEADS, HEAD_DIM), 1.0),
        'cache_swa_v': nrm(ks[5], (DEC_BATCH, DEPTH, PAST_LEN, SWA_KV_HEADS, HEAD_DIM), 1.0),
        'c': nrm(ks[6], (DEC_BATCH, D_MODEL), 1.0),
        'c_ctx': nrm(ks[7], (D_MODEL,), 0.5),
        'w_mod': nrm(ks[8], (DEPTH, D_MODEL, 6 * D_MODEL), 0.5 * D_MODEL ** -0.5),
        'b_mod': nrm(ks[9], (DEPTH, 6 * D_MODEL), 0.01),
        'norm1_g': 1.0 + nrm(ks[10], (DEPTH, D_MODEL), 0.02),
        'w_in': nrm(ks[11], (DEPTH, D_MODEL, IN_WIDTH), D_MODEL ** -0.5),
        'rpb': nrm(ks[12], (DEPTH, NA_HEADS, 2 * NA_ROWS - 1, 2 * NA_COLS - 1), 0.1),
        'sink': nrm(ks[13], (DEPTH, SWA_HEADS), 0.5),
        'out_norm_a': 1.0 + nrm(ks[14], (DEPTH, MIX_A), 0.02),
        'out_norm_b': 1.0 + nrm(ks[15], (DEPTH, MIX_B), 0.02),
        'w_out': nrm(ks[16], (DEPTH, MIX_WIDTH, D_MODEL), MIX_WIDTH ** -0.5),
        'norm2_g': 1.0 + nrm(ks[17], (DEPTH, D_MODEL), 0.02),
        'peer_wq': nrm(ks[18], (DEPTH, D_MODEL, PEER_HEADS * 2 * PEER_KEY_DIM), D_MODEL ** -0.5),
        'peer_subkeys': nrm(ks[19], (DEPTH, PEER_HEADS, 2, N_KEYS, PEER_KEY_DIM), PEER_KEY_DIM ** -0.5),
        'peer_u': nrm(ks[20], (DEPTH, N_EXPERTS, D_MODEL), D_MODEL ** -0.5),
        'peer_v': nrm(ks[21], (DEPTH, N_EXPERTS, D_MODEL), 0.5),
        'final_g': 1.0 + nrm(ks[22], (D_MODEL,), 0.02),
    }


def reference(x_prompt, x_sample, cache_na_k, cache_na_v, cache_swa_k, cache_swa_v, c, c_ctx,
              w_mod, b_mod, norm1_g, w_in, rpb, sink, out_norm_a, out_norm_b, w_out, norm2_g,
              peer_wq, peer_subkeys, peer_u, peer_v, final_g):
    xp = x_prompt
    xs = x_sample
    na_k, na_v, swa_k, swa_v = [], [], [], []
    for l in range(DEPTH):
        mod_ctx = adaln(c_ctx[None, None, :], w_mod[l], b_mod[l])
        xp, ka, va, kb, vb = context_layer(xp, mod_ctx, w_in[l], sink[l], out_norm_a[l], out_norm_b[l],
                                           w_out[l], norm1_g[l], norm2_g[l], peer_wq[l],
                                           peer_subkeys[l], peer_u[l], peer_v[l])
        na_k.append(ka)
        na_v.append(va)
        swa_k.append(kb)
        swa_v.append(vb)
        mod_lat = adaln(c[:, None, :], w_mod[l], b_mod[l])
        xs = latent_layer(xs, mod_lat, cache_na_k[:, l], cache_na_v[:, l], cache_swa_k[:, l],
                          cache_swa_v[:, l], w_in[l], rpb[l], sink[l], out_norm_a[l], out_norm_b[l],
                          w_out[l], norm1_g[l], norm2_g[l], peer_wq[l], peer_subkeys[l],
                          peer_u[l], peer_v[l])
    y_prompt = rmsnorm(xp, final_g)
    y_sample = rmsnorm(xs, final_g)
    new_na_k = jnp.stack(na_k, axis=1)
    new_na_v = jnp.stack(na_v, axis=1)
    new_swa_k = jnp.stack(swa_k, axis=1)
    new_swa_v = jnp.stack(swa_v, axis=1)
    return (y_prompt, y_sample, new_na_k, new_na_v, new_swa_k, new_swa_v)
```

```python
import functools

import jax
import jax.numpy as jnp
import numpy as np
from jax import lax
from jax.experimental import pallas as pl
from jax.experimental.pallas import tpu as pltpu

GRID_W = 64
NA_ROWS = 8
NA_COLS = 16
SWA_WINDOW = 128
SWA_BLOCK = 128
ROPE_BASE = 10000.0
PEER_TOPK = 16
EPS = 1e-6
NEG_INF = -1e30

HEAD_DIM = 128
LANES = 128
SUBLANES = 8
VMEM_LIMIT = 56 * 1024 * 1024

F32 = jnp.float32
BF16 = jnp.bfloat16


def _cparams(sem):
    return pltpu.CompilerParams(dimension_semantics=sem, vmem_limit_bytes=VMEM_LIMIT)


def _mod_index(row0, n_ctx, dec_seq):
    return jnp.where(row0 < n_ctx, 0, 1 + (row0 - n_ctx) // dec_seq)


def _adaln_kernel(c_ref, w_ref, b_ref, o_ref):
    c = c_ref[...]
    s = (c * jax.nn.sigmoid(c)).astype(BF16)
    o_ref[...] = jnp.dot(s, w_ref[...].astype(BF16), preferred_element_type=F32) + b_ref[...]


def _adaln(cond, w_mod, b_mod):
    depth, d, n = w_mod.shape
    r = cond.shape[0]
    tn = next(t for t in (512, 256, LANES) if n % t == 0)
    return pl.pallas_call(
        _adaln_kernel,
        grid=(depth, n // tn),
        in_specs=[
            pl.BlockSpec((r, d), lambda l, j: (0, 0)),
            pl.BlockSpec((None, d, tn), lambda l, j: (l, 0, j)),
            pl.BlockSpec((None, 1, tn), lambda l, j: (l, 0, j)),
        ],
        out_specs=pl.BlockSpec((None, r, tn), lambda l, j: (l, 0, j)),
        out_shape=jax.ShapeDtypeStruct((depth, r, n), F32),
        compiler_params=_cparams(("arbitrary", "arbitrary")),
        name="adaln",
    )(cond, w_mod, b_mod.reshape(depth, 1, n))


def _norm_mod_kernel(x_ref, g_ref, mod_ref, *o_refs, tr, n_ctx, dec_seq, shift_idx, scale_idx):
    x = x_ref[...]
    y = x * lax.rsqrt(jnp.mean(x * x, axis=-1, keepdims=True) + EPS) * g_ref[...]
    m = _mod_index(pl.program_id(0) * tr, n_ctx, dec_seq)
    shift = mod_ref[m, pl.ds(shift_idx, 1), :]
    scale = mod_ref[m, pl.ds(scale_idx, 1), :]
    h = y * (1.0 + scale) + shift
    for o_ref in o_refs:
        o_ref[...] = h.astype(o_ref.dtype)


def _norm_mod(x, g, mod, l, shift_idx, scale_idx, n_ctx, dec_seq, out_dtypes):
    n, d = x.shape
    tr = 256
    depth, r = mod.shape[:2]
    kern = functools.partial(_norm_mod_kernel, tr=tr, n_ctx=n_ctx, dec_seq=dec_seq,
                             shift_idx=shift_idx, scale_idx=scale_idx)
    outs = pl.pallas_call(
        kern,
        grid=(n // tr,),
        in_specs=[
            pl.BlockSpec((tr, d), lambda i: (i, 0)),
            pl.BlockSpec((None, 1, d), lambda i: (l, 0, 0)),
            pl.BlockSpec((None, r, 6, d), lambda i: (l, 0, 0, 0)),
        ],
        out_specs=[pl.BlockSpec((tr, d), lambda i: (i, 0)) for _ in out_dtypes],
        out_shape=[jax.ShapeDtypeStruct((n, d), dt) for dt in out_dtypes],
        compiler_params=_cparams(("arbitrary",)),
        name="norm_mod",
    )(x, g.reshape(depth, 1, d), mod)
    return outs


def _final_norm_kernel(x_ref, g_ref, o_ref):
    x = x_ref[...]
    o_ref[...] = x * lax.rsqrt(jnp.mean(x * x, axis=-1, keepdims=True) + EPS) * g_ref[...]


def _final_norm(x, g):
    n, d = x.shape
    tr = 256
    return pl.pallas_call(
        _final_norm_kernel,
        grid=(n // tr,),
        in_specs=[pl.BlockSpec((tr, d), lambda i: (i, 0)), pl.BlockSpec((1, d), lambda i: (0, 0))],
        out_specs=pl.BlockSpec((tr, d), lambda i: (i, 0)),
        out_shape=jax.ShapeDtypeStruct((n, d), F32),
        compiler_params=_cparams(("arbitrary",)),
        name="final_norm",
    )(x, g.reshape(1, d))


def _group_norm_kernel(oac_ref, obc_ref, oal_ref, obl_ref, ga_ref, gb_ref, h_ref, *, mix_a, ctx_tiles):
    def nrm(x, g):
        return (x * lax.rsqrt(jnp.mean(x * x, axis=-1, keepdims=True) + EPS) * g).astype(h_ref.dtype)

    def emit(oa_ref, ob_ref):
        h_ref[:, :mix_a] = nrm(oa_ref[...], ga_ref[...])
        h_ref[:, mix_a:] = nrm(ob_ref[...], gb_ref[...])

    is_ctx = pl.program_id(0) < ctx_tiles
    pl.when(is_ctx)(lambda: emit(oac_ref, obc_ref))
    pl.when(jnp.logical_not(is_ctx))(lambda: emit(oal_ref, obl_ref))


def _group_norm(oa_ctx, ob_ctx, oa_lat, ob_lat, ga, gb, l):
    n_ctx, mix_a = oa_ctx.shape
    n_lat, mix_b = ob_lat.shape
    depth = ga.shape[0]
    tr = 256
    ctx_tiles = n_ctx // tr
    lat_tiles = n_lat // tr
    ctx_map = lambda i: (jnp.minimum(i, ctx_tiles - 1), 0)
    lat_map = lambda i: (jnp.maximum(i - ctx_tiles, 0), 0)
    return pl.pallas_call(
        functools.partial(_group_norm_kernel, mix_a=mix_a, ctx_tiles=ctx_tiles),
        grid=(ctx_tiles + lat_tiles,),
        in_specs=[
            pl.BlockSpec((tr, mix_a), ctx_map),
            pl.BlockSpec((tr, mix_b), ctx_map),
            pl.BlockSpec((tr, mix_a), lat_map),
            pl.BlockSpec((tr, mix_b), lat_map),
            pl.BlockSpec((None, 1, mix_a), lambda i: (l, 0, 0)),
            pl.BlockSpec((None, 1, mix_b), lambda i: (l, 0, 0)),
        ],
        out_specs=pl.BlockSpec((tr, mix_a + mix_b), lambda i: (i, 0)),
        out_shape=jax.ShapeDtypeStruct((n_ctx + n_lat, mix_a + mix_b), BF16),
        compiler_params=_cparams(("arbitrary",)),
        name="group_norm",
    )(oa_ctx, ob_ctx, oa_lat, ob_lat, ga.reshape(depth, 1, mix_a), gb.reshape(depth, 1, mix_b))


def _mm_kernel(a_ref, w_ref, o_ref):
    o_ref[...] = jnp.dot(a_ref[...], w_ref[...].astype(BF16), preferred_element_type=F32)


def _mm_res_kernel(a_ref, w_ref, x_ref, mod_ref, o_ref, *, tm, n_ctx, dec_seq, gate_idx):
    acc = jnp.dot(a_ref[...], w_ref[...].astype(BF16), preferred_element_type=F32)
    m = _mod_index(pl.program_id(0) * tm, n_ctx, dec_seq)
    gate = mod_ref[m, pl.ds(gate_idx, 1), :]
    o_ref[...] = x_ref[...] + gate * acc


def _row_tile(n_ctx, dec_seq):
    tm = min(1024, dec_seq)
    assert n_ctx % tm == 0 and dec_seq % tm == 0
    return tm


def _col_tile(m):
    return next(t for t in (512, 256, LANES) if m % t == 0)


def _matmul(a, w, l, n_ctx, dec_seq):
    n, k = a.shape
    m = w.shape[2]
    tm = _row_tile(n_ctx, dec_seq)
    tn = _col_tile(m)
    return pl.pallas_call(
        _mm_kernel,
        grid=(n // tm, m // tn),
        in_specs=[
            pl.BlockSpec((tm, k), lambda i, j: (i, 0)),
            pl.BlockSpec((None, k, tn), lambda i, j: (l, 0, j)),
        ],
        out_specs=pl.BlockSpec((tm, tn), lambda i, j: (i, j)),
        out_shape=jax.ShapeDtypeStruct((n, m), F32),
        compiler_params=_cparams(("arbitrary", "arbitrary")),
        name="matmul",
    )(a, w)


def _matmul_residual(a, w, x, mod, l, gate_idx, n_ctx, dec_seq):
    n, k = a.shape
    m = w.shape[2]
    r = mod.shape[1]
    tm = _row_tile(n_ctx, dec_seq)
    tn = _col_tile(m)
    kern = functools.partial(_mm_res_kernel, tm=tm, n_ctx=n_ctx, dec_seq=dec_seq, gate_idx=gate_idx)
    return pl.pallas_call(
        kern,
        grid=(n // tm, m // tn),
        in_specs=[
            pl.BlockSpec((tm, k), lambda i, j: (i, 0)),
            pl.BlockSpec((None, k, tn), lambda i, j: (l, 0, j)),
            pl.BlockSpec((tm, tn), lambda i, j: (i, j)),
            pl.BlockSpec((None, r, 6, tn), lambda i, j: (l, 0, 0, j)),
        ],
        out_specs=pl.BlockSpec((tm, tn), lambda i, j: (i, j)),
        out_shape=jax.ShapeDtypeStruct((n, m), F32),
        compiler_params=_cparams(("arbitrary", "arbitrary")),
        name="matmul_residual",
    )(a, w, x, mod)


_NT = (((1,), (1,)), ((), ()))


def _qk(q, k):
    return lax.dot_general(q, k, _NT, preferred_element_type=F32)


def _pv(p, v):
    return jnp.dot(p.astype(BF16), v, preferred_element_type=F32)


def _softmax_pv(score_blocks, value_blocks, sink=None):
    m = functools.reduce(jnp.maximum, [jnp.max(s, axis=-1, keepdims=True) for s in score_blocks])
    if sink is not None:
        m = jnp.maximum(m, sink)
    ps = [jnp.exp(s - m) for s in score_blocks]
    denom = functools.reduce(jnp.add, [jnp.sum(p, axis=-1, keepdims=True) for p in ps])
    if sink is not None:
        denom = denom + jnp.exp(sink - m)
    o = functools.reduce(jnp.add, [_pv(p, v) for p, v in zip(ps, value_blocks)])
    return o / denom


def _ctx_attn_kernel(*refs, group, scale, has_sink):
    if has_sink:
        sink_ref, q_ref, k_ref, v_ref, o_ref = refs
    else:
        q_ref, k_ref, v_ref, o_ref = refs
    k = k_ref[...].astype(BF16)
    v = v_ref[...].astype(BF16)
    kv = pl.program_id(1)
    for g in range(group):
        q = q_ref[:, g * HEAD_DIM:(g + 1) * HEAD_DIM].astype(BF16)
        s = _qk(q, k) * scale
        sink = sink_ref[kv * group + g] if has_sink else None
        o_ref[:, g * HEAD_DIM:(g + 1) * HEAD_DIM] = _softmax_pv([s], [v], sink)


def _ctx_attn(p, sink_l, *, batch, seq, n_kv, group, q_col, k_col, v_col, scale):
    gw = group * HEAD_DIM
    has_sink = sink_l is not None
    kern = functools.partial(_ctx_attn_kernel, group=group, scale=scale, has_sink=has_sink)
    in_specs = [
        pl.BlockSpec((seq, gw), lambda b, h: (b, q_col // group + h)),
        pl.BlockSpec((seq, HEAD_DIM), lambda b, h: (b, k_col + h)),
        pl.BlockSpec((seq, HEAD_DIM), lambda b, h: (b, v_col + h)),
    ]
    args = [p, p, p]
    if has_sink:
        in_specs = [pl.BlockSpec(memory_space=pltpu.SMEM)] + in_specs
        args = [sink_l] + args
    return pl.pallas_call(
        kern,
        grid=(batch, n_kv),
        in_specs=in_specs,
        out_specs=pl.BlockSpec((seq, gw), lambda b, h: (b, h)),
        out_shape=jax.ShapeDtypeStruct((batch * seq, n_kv * gw), F32),
        compiler_params=_cparams(("arbitrary", "arbitrary")),
        name="ctx_attn_sink" if has_sink else "ctx_attn",
    )(*args)


def _na_groups(rows):
    kr = min(NA_ROWS, rows)
    starts = np.clip(np.arange(rows) - kr // 2, 0, rows - kr)
    groups, r = [], 0
    while r < rows:
        e = r
        while e + 1 < rows and starts[e + 1] == starts[r]:
            e += 1
        groups.append((r, e - r + 1, int(starts[r])))
        r = e + 1
    return kr, groups


def _na_bias(rpb_l, rows):
    kr = min(NA_ROWS, rows)
    r = np.arange(rows)
    rs = np.clip(r - kr // 2, 0, rows - kr)
    key_rows = rs[:, None] + np.arange(kr)[None, :]
    dr = key_rows - r[:, None] + (NA_ROWS - 1)
    cq = np.arange(GRID_W)
    cs = np.clip(cq - NA_COLS // 2, 0, GRID_W - NA_COLS)
    col_ok = (cq[None, :] >= cs[:, None]) & (cq[None, :] < cs[:, None] + NA_COLS)
    dc = np.clip(cq[None, :] - cq[:, None] + (NA_COLS - 1), 0, 2 * NA_COLS - 2)
    bias = rpb_l[:, dr[:, None, :, None], dc[None, :, None, :]]
    bias = jnp.where(col_ok[None, None, :, None, :], bias.astype(F32), NEG_INF)
    return bias.reshape(rpb_l.shape[0], rows * GRID_W, kr * GRID_W)


def _na_lat_kernel(q_ref, k_ref, v_ref, kc_ref, vc_ref, bias_ref, o_ref, *, groups, kr, scale):
    q = q_ref[...].astype(BF16)
    k = k_ref[...].astype(BF16)
    v = v_ref[...].astype(BF16)
    kc = kc_ref[...].astype(BF16)
    vc = vc_ref[...].astype(BF16)
    for r0, nr, kr0 in groups:
        q0, q1 = r0 * GRID_W, (r0 + nr) * GRID_W
        k0, k1 = kr0 * GRID_W, (kr0 + kr) * GRID_W
        qg = q[q0:q1]
        s_lat = _qk(qg, k[k0:k1]) * scale + bias_ref[q0:q1, :]
        s_ctx = _qk(qg, kc) * scale
        o_ref[q0:q1, :] = _softmax_pv([s_lat, s_ctx], [v[k0:k1], vc])


def _na_lat(p, cache_k, cache_v, bias, l, *, n_ctx, batch, t, heads, past, scale):
    rows = t // GRID_W
    kr, groups = _na_groups(rows)
    rb = n_ctx // t
    depth = cache_k.shape[1]
    ck = cache_k.reshape(batch, depth, past, heads * HEAD_DIM)
    cv = cache_v.reshape(batch, depth, past, heads * HEAD_DIM)
    kern = functools.partial(_na_lat_kernel, groups=groups, kr=kr, scale=scale)
    return pl.pallas_call(
        kern,
        grid=(heads, batch),
        in_specs=[
            pl.BlockSpec((t, HEAD_DIM), lambda h, b: (rb + b, h)),
            pl.BlockSpec((t, HEAD_DIM), lambda h, b: (rb + b, heads + h)),
            pl.BlockSpec((t, HEAD_DIM), lambda h, b: (rb + b, 2 * heads + h)),
            pl.BlockSpec((None, None, past, HEAD_DIM), lambda h, b: (b, l, 0, h)),
            pl.BlockSpec((None, None, past, HEAD_DIM), lambda h, b: (b, l, 0, h)),
            pl.BlockSpec((None, t, kr * GRID_W), lambda h, b: (h, 0, 0)),
        ],
        out_specs=pl.BlockSpec((t, HEAD_DIM), lambda h, b: (b, h)),
        out_shape=jax.ShapeDtypeStruct((batch * t, heads * HEAD_DIM), F32),
        compiler_params=_cparams(("arbitrary", "arbitrary")),
        name="na_latent",
    )(p, p, p, ck, cv, bias)


def _rope_tables(t):
    pos = np.arange(t)
    nf = HEAD_DIM // 4
    inv = jnp.asarray(ROPE_BASE, F32) ** (-jnp.arange(nf, dtype=F32) / nf)
    ang_r = jnp.asarray(pos // GRID_W, F32)[:, None] * inv
    ang_c = jnp.asarray(pos % GRID_W, F32)[:, None] * inv
    cos = jnp.concatenate([jnp.cos(ang_r)] * 2 + [jnp.cos(ang_c)] * 2, axis=-1)
    sin = jnp.concatenate([-jnp.sin(ang_r), jnp.sin(ang_r), -jnp.sin(ang_c), jnp.sin(ang_c)], axis=-1)
    return cos, sin


def _rope(x, cos, sin):
    nf = HEAD_DIM // 4
    lane = lax.broadcasted_iota(jnp.int32, x.shape, 1)
    swapped = jnp.where(lane % (2 * nf) < nf,
                        pltpu.roll(x, HEAD_DIM - nf, 1),
                        pltpu.roll(x, nf, 1))
    return x * cos + swapped * sin


def _swa_lat_kernel(sink_ref, q_ref, k_ref, v_ref, kc_ref, vc_ref, cos_ref, sin_ref, o_ref,
                    krot_ref, vbf_ref, *, group, t, scale):
    kv = pl.program_id(1)
    krot_ref[...] = _rope(k_ref[...], cos_ref[...], sin_ref[...]).astype(BF16)
    vbf_ref[...] = v_ref[...].astype(BF16)
    kc = kc_ref[...].astype(BF16)
    vc = vc_ref[...].astype(BF16)
    span = 3 * SWA_BLOCK

    def block(n, carry):
        q0 = pl.multiple_of(n * SWA_BLOCK, SWA_BLOCK)
        w0 = pl.multiple_of(jnp.clip((n - 1) * SWA_BLOCK, 0, t - span), SWA_BLOCK)
        kw = krot_ref[pl.ds(w0, span), :]
        vw = vbf_ref[pl.ds(w0, span), :]
        qpos = q0 + lax.broadcasted_iota(jnp.int32, (SWA_BLOCK, span), 0)
        kpos = w0 + lax.broadcasted_iota(jnp.int32, (SWA_BLOCK, span), 1)
        valid = jnp.abs(kpos - qpos) <= SWA_WINDOW
        cos = cos_ref[pl.ds(q0, SWA_BLOCK), :]
        sin = sin_ref[pl.ds(q0, SWA_BLOCK), :]
        for g in range(group):
            cols = slice(g * HEAD_DIM, (g + 1) * HEAD_DIM)
            qg = _rope(q_ref[pl.ds(q0, SWA_BLOCK), cols], cos, sin).astype(BF16)
            s_lat = jnp.where(valid, _qk(qg, kw) * scale, NEG_INF)
            s_ctx = _qk(qg, kc) * scale
            o_ref[pl.ds(q0, SWA_BLOCK), cols] = _softmax_pv([s_lat, s_ctx], [vw, vc],
                                                           sink_ref[kv * group + g])
        return carry

    lax.fori_loop(0, t // SWA_BLOCK, block, 0)


def _swa_lat(p, cache_k, cache_v, sink_l, cos, sin, l, *, n_ctx, batch, t, n_kv, group, past,
             q_col, k_col, v_col, scale):
    assert t >= 3 * SWA_BLOCK and t % SWA_BLOCK == 0
    gw = group * HEAD_DIM
    rb = n_ctx // t
    depth = cache_k.shape[1]
    ck = cache_k.reshape(batch, depth, past, n_kv * HEAD_DIM)
    cv = cache_v.reshape(batch, depth, past, n_kv * HEAD_DIM)
    kern = functools.partial(_swa_lat_kernel, group=group, t=t, scale=scale)
    return pl.pallas_call(
        kern,
        grid=(batch, n_kv),
        in_specs=[
            pl.BlockSpec(memory_space=pltpu.SMEM),
            pl.BlockSpec((t, gw), lambda b, h: (rb + b, q_col // group + h)),
            pl.BlockSpec((t, HEAD_DIM), lambda b, h: (rb + b, k_col + h)),
            pl.BlockSpec((t, HEAD_DIM), lambda b, h: (rb + b, v_col + h)),
            pl.BlockSpec((None, None, past, HEAD_DIM), lambda b, h: (b, l, 0, h)),
            pl.BlockSpec((None, None, past, HEAD_DIM), lambda b, h: (b, l, 0, h)),
            pl.BlockSpec((t, HEAD_DIM), lambda b, h: (0, 0)),
            pl.BlockSpec((t, HEAD_DIM), lambda b, h: (0, 0)),
        ],
        out_specs=pl.BlockSpec((t, gw), lambda b, h: (b, h)),
        out_shape=jax.ShapeDtypeStruct((batch * t, n_kv * gw), F32),
        scratch_shapes=[pltpu.VMEM((t, HEAD_DIM), BF16), pltpu.VMEM((t, HEAD_DIM), BF16)],
        compiler_params=_cparams(("arbitrary", "arbitrary")),
        name="swa_latent",
    )(sink_l, p, p, p, ck, cv, cos, sin)


def _topk_rows(v, k):
    n = v.shape[0]
    iota = lax.broadcasted_iota(jnp.int32, v.shape, 0)
    vals, idxs = [], []
    for _ in range(k):
        m = jnp.max(v, axis=0, keepdims=True)
        first = jnp.min(jnp.where(v == m, iota, n), axis=0, keepdims=True)
        v = jnp.where(iota == first, -jnp.inf, v)
        vals.append(m)
        idxs.append(first)
    return jnp.concatenate(vals, axis=0), jnp.concatenate(idxs, axis=0)


def _peer_topk_kernel(q_ref, sk_ref, idx_ref, g_ref, *, n_keys, key_dim):
    k = PEER_TOPK
    sv, si = [], []
    for part in range(2):
        q = q_ref[:, part * key_dim:(part + 1) * key_dim].astype(BF16)
        s = _qk(sk_ref[part].astype(BF16), q)
        v, i = _topk_rows(s, k)
        sv.append(v)
        si.append(i)
    cand = jnp.concatenate([sv[0][a:a + 1] + sv[1] for a in range(k)], axis=0)
    cv, cp = _topk_rows(cand, k)
    ca, cb = cp // k, cp % k
    e1 = jnp.zeros_like(cp)
    e2 = jnp.zeros_like(cp)
    for a in range(k):
        e1 = jnp.where(ca == a, si[0][a:a + 1], e1)
        e2 = jnp.where(cb == a, si[1][a:a + 1], e2)
    idx_ref[...] = e1 * n_keys + e2
    e = jnp.exp(cv - jnp.max(cv, axis=0, keepdims=True))
    g_ref[...] = e / jnp.sum(e, axis=0, keepdims=True)


def _peer_topk(q, subkeys, l):
    n = q.shape[0]
    depth, heads, _, n_keys, key_dim = subkeys.shape
    tq = 256
    k = PEER_TOPK
    kern = functools.partial(_peer_topk_kernel, n_keys=n_keys, key_dim=key_dim)
    idx, g = pl.pallas_call(
        kern,
        grid=(n // tq, heads),
        in_specs=[
            pl.BlockSpec((tq, 2 * key_dim), lambda i, h: (i, h)),
            pl.BlockSpec((None, None, 2, n_keys, key_dim), lambda i, h: (l, h, 0, 0, 0)),
        ],
        out_specs=[pl.BlockSpec((None, k, tq), lambda i, h: (h, 0, i))] * 2,
        out_shape=[jax.ShapeDtypeStruct((heads, k, n), jnp.int32),
                   jax.ShapeDtypeStruct((heads, k, n), F32)],
        compiler_params=_cparams(("arbitrary", "arbitrary")),
        name="peer_topk",
    )(q, subkeys)
    return idx.reshape(heads * k, n), g.reshape(heads * k, n)


PEER_TOKENS = 32
PEER_SLOTS = 4


def _peer_kernel(idx_ref, g_ref, x_ref, res_ref, mod_ref, u_hbm, v_hbm, o_ref,
                 ubuf, vbuf, hbuf, sem_u, sem_v, *, l, kk, n_ctx, dec_seq, gate_idx):
    tb, ns = PEER_TOKENS, PEER_SLOTS
    step = pl.program_id(0)
    lane0 = (step % (LANES // tb)) * tb
    gate = mod_ref[_mod_index(step * tb, n_ctx, dec_seq), pl.ds(gate_idx, 1), :]

    def row_copy(hbm, buf, sem, e, slot, k):
        return pltpu.make_async_copy(hbm.at[l, pl.ds(e, 1), :], buf.at[slot, pl.ds(k, 1), :], sem.at[slot])

    def issue(t, slot):
        def body(k, c):
            e = idx_ref[t, k]
            row_copy(u_hbm, ubuf, sem_u, e, slot, k).start()
            row_copy(v_hbm, vbuf, sem_v, e, slot, k).start()
            return c
        lax.fori_loop(0, kk, body, 0)

    def wait_all(hbm, buf, sem, slot):
        def body(k, c):
            row_copy(hbm, buf, sem, 0, slot, k).wait()
            return c
        lax.fori_loop(0, kk, body, 0)

    for t in range(ns - 1):
        issue(t, t)

    def token(t, carry):
        slot = t % ns
        nxt = t + ns - 1
        pl.when(nxt < tb)(lambda: issue(nxt, nxt % ns))
        xb = x_ref[pl.ds(t, 1), :]

        wait_all(u_hbm, ubuf, sem_u, slot)

        def u_group(kg, c):
            k0 = pl.multiple_of(kg * SUBLANES, SUBLANES)
            rows = ubuf[slot, pl.ds(k0, SUBLANES), :]
            hk = jnp.sum(rows * xb, axis=-1, keepdims=True)
            hbuf[pl.ds(k0, SUBLANES), :] = jnp.broadcast_to(hk, (SUBLANES, LANES))
            return c
        lax.fori_loop(0, kk // SUBLANES, u_group, 0)

        lane = lax.broadcasted_iota(jnp.int32, (kk, LANES), 1)
        gcol = jnp.sum(jnp.where(lane == lane0 + t, g_ref[...], 0.0), axis=1, keepdims=True)
        h = hbuf[...]
        hbuf[...] = gcol * (0.5 * h * (1.0 + lax.erf(h * np.float32(0.5 ** 0.5))))

        wait_all(v_hbm, vbuf, sem_v, slot)

        def v_group(kg, acc):
            k0 = pl.multiple_of(kg * SUBLANES, SUBLANES)
            rows = vbuf[slot, pl.ds(k0, SUBLANES), :]
            return acc + rows * hbuf[pl.ds(k0, SUBLANES), 0:1]
        acc = lax.fori_loop(0, kk // SUBLANES, v_group, jnp.zeros((SUBLANES, x_ref.shape[1]), F32))
        y = jnp.sum(acc, axis=0, keepdims=True)
        o_ref[pl.ds(t, 1), :] = res_ref[pl.ds(t, 1), :] + gate * y
        return carry

    lax.fori_loop(0, tb, token, 0)


def _peer(idx, g, x, res, mod, u, v, l, gate_idx, n_ctx, dec_seq):
    n, d = x.shape
    kk = idx.shape[1]
    r = mod.shape[1]
    tb, ns = PEER_TOKENS, PEER_SLOTS
    assert kk % SUBLANES == 0 and LANES % tb == 0 and n % LANES == 0
    kern = functools.partial(_peer_kernel, l=l, kk=kk, n_ctx=n_ctx, dec_seq=dec_seq, gate_idx=gate_idx)
    return pl.pallas_call(
        kern,
        grid=(n // tb,),
        in_specs=[
            pl.BlockSpec((tb, kk), lambda i: (i, 0), memory_space=pltpu.SMEM),
            pl.BlockSpec((kk, LANES), lambda i: (0, i // (LANES // tb))),
            pl.BlockSpec((tb, d), lambda i: (i, 0)),
            pl.BlockSpec((tb, d), lambda i: (i, 0)),
            pl.BlockSpec((None, r, 6, d), lambda i: (l, 0, 0, 0)),
            pl.BlockSpec(memory_space=pl.ANY),
            pl.BlockSpec(memory_space=pl.ANY),
        ],
        out_specs=pl.BlockSpec((tb, d), lambda i: (i, 0)),
        out_shape=jax.ShapeDtypeStruct((n, d), F32),
        scratch_shapes=[
            pltpu.VMEM((ns, kk, d), F32),
            pltpu.VMEM((ns, kk, d), F32),
            pltpu.VMEM((kk, LANES), F32),
            pltpu.SemaphoreType.DMA((ns,)),
            pltpu.SemaphoreType.DMA((ns,)),
        ],
        compiler_params=_cparams(("arbitrary",)),
        name="peer_experts",
    )(idx, g, x, res, mod, u, v)


def kernel(x_prompt, x_sample, cache_na_k, cache_na_v, cache_swa_k, cache_swa_v, c, c_ctx,
           w_mod, b_mod, norm1_g, w_in, rpb, sink, out_norm_a, out_norm_b, w_out, norm2_g,
           peer_wq, peer_subkeys, peer_u, peer_v, final_g):
    batch, seq, d = x_prompt.shape
    dec_batch, dec_seq, _ = x_sample.shape
    depth = w_mod.shape[0]
    past = cache_na_k.shape[2]
    na_heads = cache_na_k.shape[3]
    swa_kv = cache_swa_k.shape[3]
    swa_heads = sink.shape[1]
    group = swa_heads // swa_kv
    n_ctx = batch * seq
    scale = HEAD_DIM ** -0.5
    qa_col, ka_col, va_col = 0, na_heads, 2 * na_heads
    qb_col = 3 * na_heads
    kb_col = qb_col + swa_heads
    vb_col = kb_col + swa_kv

    x = jnp.concatenate([x_prompt.reshape(n_ctx, d), x_sample.reshape(dec_batch * dec_seq, d)], axis=0)

    n_mod = 1 + dec_batch
    r_mod = -(-n_mod // SUBLANES) * SUBLANES
    cond = jnp.concatenate([c_ctx[None, :], c, jnp.zeros((r_mod - n_mod, d), F32)], axis=0)
    mod = _adaln(cond, w_mod, b_mod).reshape(depth, r_mod, 6, d)

    cos, sin = _rope_tables(dec_seq)
    na_k, na_v, swa_k, swa_v = [], [], [], []
    for l in range(depth):
        (h,) = _norm_mod(x, norm1_g, mod, l, 0, 1, n_ctx, dec_seq, [BF16])
        p = _matmul(h, w_in, l, n_ctx, dec_seq)
        oa_ctx = _ctx_attn(p, None, batch=batch, seq=seq, n_kv=na_heads, group=1,
                           q_col=qa_col, k_col=ka_col, v_col=va_col, scale=scale)
        ob_ctx = _ctx_attn(p, sink[l], batch=batch, seq=seq, n_kv=swa_kv, group=group,
                           q_col=qb_col, k_col=kb_col, v_col=vb_col, scale=scale)
        oa_lat = _na_lat(p, cache_na_k, cache_na_v, _na_bias(rpb[l], dec_seq // GRID_W), l,
                         n_ctx=n_ctx, batch=dec_batch, t=dec_seq, heads=na_heads, past=past, scale=scale)
        ob_lat = _swa_lat(p, cache_swa_k, cache_swa_v, sink[l], cos, sin, l,
                          n_ctx=n_ctx, batch=dec_batch, t=dec_seq, n_kv=swa_kv, group=group, past=past,
                          q_col=qb_col, k_col=kb_col, v_col=vb_col, scale=scale)
        hn = _group_norm(oa_ctx, ob_ctx, oa_lat, ob_lat, out_norm_a, out_norm_b, l)
        x = _matmul_residual(hn, w_out, x, mod, l, 2, n_ctx, dec_seq)

        h2_bf, h2 = _norm_mod(x, norm2_g, mod, l, 3, 4, n_ctx, dec_seq, [BF16, F32])
        q = _matmul(h2_bf, peer_wq, l, n_ctx, dec_seq)
        idx_t, g_t = _peer_topk(q, peer_subkeys, l)
        x = _peer(idx_t.T, g_t, h2, x, mod, peer_u, peer_v, l, 5, n_ctx, dec_seq)

        pc = p[:n_ctx]
        na_k.append(pc[:, ka_col * HEAD_DIM:va_col * HEAD_DIM].reshape(batch, seq, na_heads, HEAD_DIM))
        na_v.append(pc[:, va_col * HEAD_DIM:qb_col * HEAD_DIM].reshape(batch, seq, na_heads, HEAD_DIM))
        swa_k.append(pc[:, kb_col * HEAD_DIM:vb_col * HEAD_DIM].reshape(batch, seq, swa_kv, HEAD_DIM))
        swa_v.append(pc[:, vb_col * HEAD_DIM:].reshape(batch, seq, swa_kv, HEAD_DIM))

    y = _final_norm(x, final_g)
    return (y[:n_ctx].reshape(batch, seq, d), y[n_ctx:].reshape(dec_batch, dec_seq, d),
            jnp.stack(na_k, axis=1), jnp.stack(na_v, axis=1),
            jnp.stack(swa_k, axis=1), jnp.stack(swa_v, axis=1))
```

```python
import functools

import jax
import jax.numpy as jnp
import numpy as np
from jax import lax
from jax.experimental import pallas as pl
from jax.experimental.pallas import tpu as pltpu

GRID_W = 64
NA_ROWS = 8
NA_COLS = 16
SWA_WINDOW = 128
SWA_BLOCK = 128
ROPE_BASE = 10000.0
PEER_TOPK = 16
EPS = 1e-6
NEG_INF = -1e30

HEAD_DIM = 128
LANES = 128
SUBLANES = 8
VMEM_LIMIT = 56 * 1024 * 1024

F32 = jnp.float32
BF16 = jnp.bfloat16


def _cparams(sem):
    return pltpu.CompilerParams(dimension_semantics=sem, vmem_limit_bytes=VMEM_LIMIT)


def _mod_index(row0, n_ctx, dec_seq):
    return jnp.where(row0 < n_ctx, 0, 1 + (row0 - n_ctx) // dec_seq)


def _adaln_kernel(c_ref, w_ref, b_ref, o_ref):
    c = c_ref[...]
    s = (c * jax.nn.sigmoid(c)).astype(BF16)
    o_ref[...] = jnp.dot(s, w_ref[...].astype(BF16), preferred_element_type=F32) + b_ref[...]


def _adaln(cond, w_mod, b_mod):
    depth, d, n = w_mod.shape
    r = cond.shape[0]
    tn = next(t for t in (512, 256, LANES) if n % t == 0)
    return pl.pallas_call(
        _adaln_kernel,
        grid=(depth, n // tn),
        in_specs=[
            pl.BlockSpec((r, d), lambda l, j: (0, 0)),
            pl.BlockSpec((None, d, tn), lambda l, j: (l, 0, j)),
            pl.BlockSpec((None, 1, tn), lambda l, j: (l, 0, j)),
        ],
        out_specs=pl.BlockSpec((None, r, tn), lambda l, j: (l, 0, j)),
        out_shape=jax.ShapeDtypeStruct((depth, r, n), F32),
        compiler_params=_cparams(("arbitrary", "arbitrary")),
        name="adaln",
    )(cond, w_mod, b_mod.reshape(depth, 1, n))


def _norm_mod_kernel(x_ref, g_ref, mod_ref, *o_refs, tr, n_ctx, dec_seq, shift_idx, scale_idx):
    x = x_ref[...]
    y = x * lax.rsqrt(jnp.mean(x * x, axis=-1, keepdims=True) + EPS) * g_ref[...]
    m = _mod_index(pl.program_id(0) * tr, n_ctx, dec_seq)
    shift = mod_ref[m, pl.ds(shift_idx, 1), :]
    scale = mod_ref[m, pl.ds(scale_idx, 1), :]
    h = y * (1.0 + scale) + shift
    for o_ref in o_refs:
        o_ref[...] = h.astype(o_ref.dtype)


def _norm_mod(x, g, mod, l, shift_idx, scale_idx, n_ctx, dec_seq, out_dtypes):
    n, d = x.shape
    tr = 256
    depth, r = mod.shape[:2]
    kern = functools.partial(_norm_mod_kernel, tr=tr, n_ctx=n_ctx, dec_seq=dec_seq,
                             shift_idx=shift_idx, scale_idx=scale_idx)
    outs = pl.pallas_call(
        kern,
        grid=(n // tr,),
        in_specs=[
            pl.BlockSpec((tr, d), lambda i: (i, 0)),
            pl.BlockSpec((None, 1, d), lambda i: (l, 0, 0)),
            pl.BlockSpec((None, r, 6, d), lambda i: (l, 0, 0, 0)),
        ],
        out_specs=[pl.BlockSpec((tr, d), lambda i: (i, 0)) for _ in out_dtypes],
        out_shape=[jax.ShapeDtypeStruct((n, d), dt) for dt in out_dtypes],
        compiler_params=_cparams(("arbitrary",)),
        name="norm_mod",
    )(x, g.reshape(depth, 1, d), mod)
    return outs


def _final_norm_kernel(x_ref, g_ref, o_ref):
    x = x_ref[...]
    o_ref[...] = x * lax.rsqrt(jnp.mean(x * x, axis=-1, keepdims=True) + EPS) * g_ref[...]


def _final_norm(x, g):
    n, d = x.shape
    tr = 256
    return pl.pallas_call(
        _final_norm_kernel,
        grid=(n // tr,),
        in_specs=[pl.BlockSpec((tr, d), lambda i: (i, 0)), pl.BlockSpec((1, d), lambda i: (0, 0))],
        out_specs=pl.BlockSpec((tr, d), lambda i: (i, 0)),
        out_shape=jax.ShapeDtypeStruct((n, d), F32),
        compiler_params=_cparams(("arbitrary",)),
        name="final_norm",
    )(x, g.reshape(1, d))


def _group_norm_kernel(oac_ref, obc_ref, oal_ref, obl_ref, ga_ref, gb_ref, h_ref, *, mix_a, ctx_tiles):
    def nrm(x, g):
        return (x * lax.rsqrt(jnp.mean(x * x, axis=-1, keepdims=True) + EPS) * g).astype(h_ref.dtype)

    def emit(oa_ref, ob_ref):
        h_ref[:, :mix_a] = nrm(oa_ref[...], ga_ref[...])
        h_ref[:, mix_a:] = nrm(ob_ref[...], gb_ref[...])

    is_ctx = pl.program_id(0) < ctx_tiles
    pl.when(is_ctx)(lambda: emit(oac_ref, obc_ref))
    pl.when(jnp.logical_not(is_ctx))(lambda: emit(oal_ref, obl_ref))


def _group_norm(oa_ctx, ob_ctx, oa_lat, ob_lat, ga, gb, l):
    n_ctx, mix_a = oa_ctx.shape
    n_lat, mix_b = ob_lat.shape
    depth = ga.shape[0]
    tr = 256
    ctx_tiles = n_ctx // tr
    lat_tiles = n_lat // tr
    ctx_map = lambda i: (jnp.minimum(i, ctx_tiles - 1), 0)
    lat_map = lambda i: (jnp.maximum(i - ctx_tiles, 0), 0)
    return pl.pallas_call(
        functools.partial(_group_norm_kernel, mix_a=mix_a, ctx_tiles=ctx_tiles),
        grid=(ctx_tiles + lat_tiles,),
        in_specs=[
            pl.BlockSpec((tr, mix_a), ctx_map),
            pl.BlockSpec((tr, mix_b), ctx_map),
            pl.BlockSpec((tr, mix_a), lat_map),
            pl.BlockSpec((tr, mix_b), lat_map),
            pl.BlockSpec((None, 1, mix_a), lambda i: (l, 0, 0)),
            pl.BlockSpec((None, 1, mix_b), lambda i: (l, 0, 0)),
        ],
        out_specs=pl.BlockSpec((tr, mix_a + mix_b), lambda i: (i, 0)),
        out_shape=jax.ShapeDtypeStruct((n_ctx + n_lat, mix_a + mix_b), BF16),
        compiler_params=_cparams(("arbitrary",)),
        name="group_norm",
    )(oa_ctx, ob_ctx, oa_lat, ob_lat, ga.reshape(depth, 1, mix_a), gb.reshape(depth, 1, mix_b))


def _mm_kernel(a_ref, w_ref, o_ref):
    o_ref[...] = jnp.dot(a_ref[...], w_ref[...].astype(BF16), preferred_element_type=F32)


def _mm_res_kernel(a_ref, w_ref, x_ref, mod_ref, o_ref, *, tm, n_ctx, dec_seq, gate_idx):
    acc = jnp.dot(a_ref[...], w_ref[...].astype(BF16), preferred_element_type=F32)
    m = _mod_index(pl.program_id(0) * tm, n_ctx, dec_seq)
    gate = mod_ref[m, pl.ds(gate_idx, 1), :]
    o_ref[...] = x_ref[...] + gate * acc


def _row_tile(n_ctx, dec_seq):
    tm = min(1024, dec_seq)
    assert n_ctx % tm == 0 and dec_seq % tm == 0
    return tm


def _col_tile(m):
    return next(t for t in (512, 256, LANES) if m % t == 0)


def _matmul(a, w, l, n_ctx, dec_seq):
    n, k = a.shape
    m = w.shape[2]
    tm = _row_tile(n_ctx, dec_seq)
    tn = _col_tile(m)
    return pl.pallas_call(
        _mm_kernel,
        grid=(n // tm, m // tn),
        in_specs=[
            pl.BlockSpec((tm, k), lambda i, j: (i, 0)),
            pl.BlockSpec((None, k, tn), lambda i, j: (l, 0, j)),
        ],
        out_specs=pl.BlockSpec((tm, tn), lambda i, j: (i, j)),
        out_shape=jax.ShapeDtypeStruct((n, m), F32),
        compiler_params=_cparams(("arbitrary", "arbitrary")),
        name="matmul",
    )(a, w)


def _matmul_residual(a, w, x, mod, l, gate_idx, n_ctx, dec_seq):
    n, k = a.shape
    m = w.shape[2]
    r = mod.shape[1]
    tm = _row_tile(n_ctx, dec_seq)
    tn = _col_tile(m)
    kern = functools.partial(_mm_res_kernel, tm=tm, n_ctx=n_ctx, dec_seq=dec_seq, gate_idx=gate_idx)
    return pl.pallas_call(
        kern,
        grid=(n // tm, m // tn),
        in_specs=[
            pl.BlockSpec((tm, k), lambda i, j: (i, 0)),
            pl.BlockSpec((None, k, tn), lambda i, j: (l, 0, j)),
            pl.BlockSpec((tm, tn), lambda i, j: (i, j)),
            pl.BlockSpec((None, r, 6, tn), lambda i, j: (l, 0, 0, j)),
        ],
        out_specs=pl.BlockSpec((tm, tn), lambda i, j: (i, j)),
        out_shape=jax.ShapeDtypeStruct((n, m), F32),
        compiler_params=_cparams(("arbitrary", "arbitrary")),
        name="matmul_residual",
    )(a, w, x, mod)


_NT = (((1,), (1,)), ((), ()))


def _qk(q, k):
    return lax.dot_general(q, k, _NT, preferred_element_type=F32)


def _pv(p, v):
    return jnp.dot(p.astype(BF16), v, preferred_element_type=F32)


def _softmax_pv(score_blocks, value_blocks, sink=None):
    m = functools.reduce(jnp.maximum, [jnp.max(s, axis=-1, keepdims=True) for s in score_blocks])
    if sink is not None:
        m = jnp.maximum(m, sink)
    ps = [jnp.exp(s - m) for s in score_blocks]
    denom = functools.reduce(jnp.add, [jnp.sum(p, axis=-1, keepdims=True) for p in ps])
    if sink is not None:
        denom = denom + jnp.exp(sink - m)
    o = functools.reduce(jnp.add, [_pv(p, v) for p, v in zip(ps, value_blocks)])
    return o / denom


def _ctx_attn_kernel(*refs, group, scale, has_sink):
    if has_sink:
        sink_ref, q_ref, k_ref, v_ref, o_ref = refs
    else:
        q_ref, k_ref, v_ref, o_ref = refs
    k = k_ref[...].astype(BF16)
    v = v_ref[...].astype(BF16)
    kv = pl.program_id(1)
    for g in range(group):
        q = q_ref[:, g * HEAD_DIM:(g + 1) * HEAD_DIM].astype(BF16)
        s = _qk(q, k) * scale
        sink = sink_ref[kv * group + g] if has_sink else None
        o_ref[:, g * HEAD_DIM:(g + 1) * HEAD_DIM] = _softmax_pv([s], [v], sink)


def _ctx_attn(p, sink_l, *, batch, seq, n_kv, group, q_col, k_col, v_col, scale):
    gw = group * HEAD_DIM
    has_sink = sink_l is not None
    kern = functools.partial(_ctx_attn_kernel, group=group, scale=scale, has_sink=has_sink)
    in_specs = [
        pl.BlockSpec((seq, gw), lambda b, h: (b, q_col // group + h)),
        pl.BlockSpec((seq, HEAD_DIM), lambda b, h: (b, k_col + h)),
        pl.BlockSpec((seq, HEAD_DIM), lambda b, h: (b, v_col + h)),
    ]
    args = [p, p, p]
    if has_sink:
        in_specs = [pl.BlockSpec(memory_space=pltpu.SMEM)] + in_specs
        args = [sink_l] + args
    return pl.pallas_call(
        kern,
        grid=(batch, n_kv),
        in_specs=in_specs,
        out_specs=pl.BlockSpec((seq, gw), lambda b, h: (b, h)),
        out_shape=jax.ShapeDtypeStruct((batch * seq, n_kv * gw), F32),
        compiler_params=_cparams(("arbitrary", "arbitrary")),
        name="ctx_attn_sink" if has_sink else "ctx_attn",
    )(*args)


def _na_groups(rows):
    kr = min(NA_ROWS, rows)
    starts = np.clip(np.arange(rows) - kr // 2, 0, rows - kr)
    groups, r = [], 0
    while r < rows:
        e = r
        while e + 1 < rows and starts[e + 1] == starts[r]:
            e += 1
        groups.append((r, e - r + 1, int(starts[r])))
        r = e + 1
    return kr, groups


def _na_bias(rpb, rows):
    depth, heads = rpb.shape[:2]
    kr = min(NA_ROWS, rows)
    r = np.arange(rows)
    d0 = np.clip(r - kr // 2, 0, rows - kr) - r + (NA_ROWS - 1)
    by_row = jnp.stack([rpb[:, :, int(s):int(s) + kr, :] for s in d0], axis=2)
    cq = np.arange(GRID_W)
    cs = np.clip(cq - NA_COLS // 2, 0, GRID_W - NA_COLS)
    col_ok = (cq[None, :] >= cs[:, None]) & (cq[None, :] < cs[:, None] + NA_COLS)
    dc = np.clip(cq[None, :] - cq[:, None] + (NA_COLS - 1), 0, 2 * NA_COLS - 2)
    onehot = (dc[None, :, :] == np.arange(2 * NA_COLS - 1)[:, None, None]).astype(np.float32)
    bias = jnp.einsum('lhrij,jqk->lhrqik', by_row.astype(F32), onehot, precision=lax.Precision.HIGHEST)
    bias = jnp.where(col_ok[None, None, None, :, None, :], bias, NEG_INF)
    return bias.reshape(depth, heads, rows * GRID_W, kr * GRID_W)


def _na_lat_kernel(q_ref, k_ref, v_ref, kc_ref, vc_ref, bias_ref, o_ref, *, groups, kr, scale):
    q = q_ref[...].astype(BF16)
    k = k_ref[...].astype(BF16)
    v = v_ref[...].astype(BF16)
    kc = kc_ref[...].astype(BF16)
    vc = vc_ref[...].astype(BF16)
    for r0, nr, kr0 in groups:
        q0, q1 = r0 * GRID_W, (r0 + nr) * GRID_W
        k0, k1 = kr0 * GRID_W, (kr0 + kr) * GRID_W
        qg = q[q0:q1]
        s_lat = _qk(qg, k[k0:k1]) * scale + bias_ref[q0:q1, :]
        s_ctx = _qk(qg, kc) * scale
        o_ref[q0:q1, :] = _softmax_pv([s_lat, s_ctx], [v[k0:k1], vc])


def _na_lat(p, cache_k, cache_v, bias, l, *, n_ctx, batch, t, heads, past, scale):
    rows = t // GRID_W
    kr, groups = _na_groups(rows)
    rb = n_ctx // t
    depth = cache_k.shape[1]
    ck = cache_k.reshape(batch, depth, past, heads * HEAD_DIM)
    cv = cache_v.reshape(batch, depth, past, heads * HEAD_DIM)
    kern = functools.partial(_na_lat_kernel, groups=groups, kr=kr, scale=scale)
    return pl.pallas_call(
        kern,
        grid=(heads, batch),
        in_specs=[
            pl.BlockSpec((t, HEAD_DIM), lambda h, b: (rb + b, h)),
            pl.BlockSpec((t, HEAD_DIM), lambda h, b: (rb + b, heads + h)),
            pl.BlockSpec((t, HEAD_DIM), lambda h, b: (rb + b, 2 * heads + h)),
            pl.BlockSpec((None, None, past, HEAD_DIM), lambda h, b: (b, l, 0, h)),
            pl.BlockSpec((None, None, past, HEAD_DIM), lambda h, b: (b, l, 0, h)),
            pl.BlockSpec((None, None, t, kr * GRID_W), lambda h, b: (l, h, 0, 0)),
        ],
        out_specs=pl.BlockSpec((t, HEAD_DIM), lambda h, b: (b, h)),
        out_shape=jax.ShapeDtypeStruct((batch * t, heads * HEAD_DIM), F32),
        compiler_params=_cparams(("arbitrary", "arbitrary")),
        name="na_latent",
    )(p, p, p, ck, cv, bias)


def _rope_tables(t):
    pos = np.arange(t)
    nf = HEAD_DIM // 4
    inv = jnp.asarray(ROPE_BASE, F32) ** (-jnp.arange(nf, dtype=F32) / nf)
    ang_r = jnp.asarray(pos // GRID_W, F32)[:, None] * inv
    ang_c = jnp.asarray(pos % GRID_W, F32)[:, None] * inv
    cos = jnp.concatenate([jnp.cos(ang_r)] * 2 + [jnp.cos(ang_c)] * 2, axis=-1)
    sin = jnp.concatenate([-jnp.sin(ang_r), jnp.sin(ang_r), -jnp.sin(ang_c), jnp.sin(ang_c)], axis=-1)
    return cos, sin


def _rope(x, cos, sin):
    nf = HEAD_DIM // 4
    lane = lax.broadcasted_iota(jnp.int32, x.shape, 1)
    swapped = jnp.where(lane % (2 * nf) < nf,
                        pltpu.roll(x, HEAD_DIM - nf, 1),
                        pltpu.roll(x, nf, 1))
    return x * cos + swapped * sin


def _swa_lat_kernel(sink_ref, q_ref, k_ref, v_ref, kc_ref, vc_ref, cos_ref, sin_ref, o_ref,
                    krot_ref, vbf_ref, *, group, t, scale):
    kv = pl.program_id(1)
    krot_ref[...] = _rope(k_ref[...], cos_ref[...], sin_ref[...]).astype(BF16)
    vbf_ref[...] = v_ref[...].astype(BF16)
    kc = kc_ref[...].astype(BF16)
    vc = vc_ref[...].astype(BF16)
    span = 3 * SWA_BLOCK

    def block(n, carry):
        q0 = pl.multiple_of(n * SWA_BLOCK, SWA_BLOCK)
        w0 = pl.multiple_of(jnp.clip((n - 1) * SWA_BLOCK, 0, t - span), SWA_BLOCK)
        kw = krot_ref[pl.ds(w0, span), :]
        vw = vbf_ref[pl.ds(w0, span), :]
        qpos = q0 + lax.broadcasted_iota(jnp.int32, (SWA_BLOCK, span), 0)
        kpos = w0 + lax.broadcasted_iota(jnp.int32, (SWA_BLOCK, span), 1)
        valid = jnp.abs(kpos - qpos) <= SWA_WINDOW
        cos = cos_ref[pl.ds(q0, SWA_BLOCK), :]
        sin = sin_ref[pl.ds(q0, SWA_BLOCK), :]
        for g in range(group):
            cols = slice(g * HEAD_DIM, (g + 1) * HEAD_DIM)
            qg = _rope(q_ref[pl.ds(q0, SWA_BLOCK), cols], cos, sin).astype(BF16)
            s_lat = jnp.where(valid, _qk(qg, kw) * scale, NEG_INF)
            s_ctx = _qk(qg, kc) * scale
            o_ref[pl.ds(q0, SWA_BLOCK), cols] = _softmax_pv([s_lat, s_ctx], [vw, vc],
                                                           sink_ref[kv * group + g])
        return carry

    lax.fori_loop(0, t // SWA_BLOCK, block, 0)


def _swa_lat(p, cache_k, cache_v, sink_l, cos, sin, l, *, n_ctx, batch, t, n_kv, group, past,
             q_col, k_col, v_col, scale):
    assert t >= 3 * SWA_BLOCK and t % SWA_BLOCK == 0
    gw = group * HEAD_DIM
    rb = n_ctx // t
    depth = cache_k.shape[1]
    ck = cache_k.reshape(batch, depth, past, n_kv * HEAD_DIM)
    cv = cache_v.reshape(batch, depth, past, n_kv * HEAD_DIM)
    kern = functools.partial(_swa_lat_kernel, group=group, t=t, scale=scale)
    return pl.pallas_call(
        kern,
        grid=(batch, n_kv),
        in_specs=[
            pl.BlockSpec(memory_space=pltpu.SMEM),
            pl.BlockSpec((t, gw), lambda b, h: (rb + b, q_col // group + h)),
            pl.BlockSpec((t, HEAD_DIM), lambda b, h: (rb + b, k_col + h)),
            pl.BlockSpec((t, HEAD_DIM), lambda b, h: (rb + b, v_col + h)),
            pl.BlockSpec((None, None, past, HEAD_DIM), lambda b, h: (b, l, 0, h)),
            pl.BlockSpec((None, None, past, HEAD_DIM), lambda b, h: (b, l, 0, h)),
            pl.BlockSpec((t, HEAD_DIM), lambda b, h: (0, 0)),
            pl.BlockSpec((t, HEAD_DIM), lambda b, h: (0, 0)),
        ],
        out_specs=pl.BlockSpec((t, gw), lambda b, h: (b, h)),
        out_shape=jax.ShapeDtypeStruct((batch * t, n_kv * gw), F32),
        scratch_shapes=[pltpu.VMEM((t, HEAD_DIM), BF16), pltpu.VMEM((t, HEAD_DIM), BF16)],
        compiler_params=_cparams(("arbitrary", "arbitrary")),
        name="swa_latent",
    )(sink_l, p, p, p, ck, cv, cos, sin)


def _topk_rows(v, k):
    n = v.shape[0]
    iota = lax.broadcasted_iota(jnp.int32, v.shape, 0)
    vals, idxs = [], []
    for _ in range(k):
        m = jnp.max(v, axis=0, keepdims=True)
        first = jnp.min(jnp.where(v == m, iota, n), axis=0, keepdims=True)
        v = jnp.where(iota == first, -jnp.inf, v)
        vals.append(m)
        idxs.append(first)
    return jnp.concatenate(vals, axis=0), jnp.concatenate(idxs, axis=0)


def _peer_topk_kernel(q_ref, sk_ref, idx_ref, g_ref, *, n_keys, key_dim):
    k = PEER_TOPK
    sv, si = [], []
    for part in range(2):
        q = q_ref[:, part * key_dim:(part + 1) * key_dim].astype(BF16)
        s = _qk(sk_ref[part].astype(BF16), q)
        v, i = _topk_rows(s, k)
        sv.append(v)
        si.append(i)
    cand = jnp.concatenate([sv[0][a:a + 1] + sv[1] for a in range(k)], axis=0)
    cv, cp = _topk_rows(cand, k)
    ca, cb = cp // k, cp % k
    e1 = jnp.zeros_like(cp)
    e2 = jnp.zeros_like(cp)
    for a in range(k):
        e1 = jnp.where(ca == a, si[0][a:a + 1], e1)
        e2 = jnp.where(cb == a, si[1][a:a + 1], e2)
    idx_ref[...] = e1 * n_keys + e2
    e = jnp.exp(cv - jnp.max(cv, axis=0, keepdims=True))
    g_ref[...] = e / jnp.sum(e, axis=0, keepdims=True)


def _peer_topk(q, subkeys, l):
    n = q.shape[0]
    depth, heads, _, n_keys, key_dim = subkeys.shape
    tq = 256
    k = PEER_TOPK
    kern = functools.partial(_peer_topk_kernel, n_keys=n_keys, key_dim=key_dim)
    idx, g = pl.pallas_call(
        kern,
        grid=(n // tq, heads),
        in_specs=[
            pl.BlockSpec((tq, 2 * key_dim), lambda i, h: (i, h)),
            pl.BlockSpec((None, None, 2, n_keys, key_dim), lambda i, h: (l, h, 0, 0, 0)),
        ],
        out_specs=[pl.BlockSpec((None, k, tq), lambda i, h: (h, 0, i))] * 2,
        out_shape=[jax.ShapeDtypeStruct((heads, k, n), jnp.int32),
                   jax.ShapeDtypeStruct((heads, k, n), F32)],
        compiler_params=_cparams(("arbitrary", "arbitrary")),
        name="peer_topk",
    )(q, subkeys)
    return idx.reshape(heads * k, n), g.reshape(heads * k, n)


PEER_TOKENS = 64
PEER_SLOTS = 4
PEER_DTYPE = BF16


def _expert_tables(w):
    depth, e, d = w.shape
    return w.astype(PEER_DTYPE).reshape(depth, e, d // LANES, LANES)


def _peer_kernel(idx_ref, g_ref, x_ref, res_ref, mod_ref, u_hbm, v_hbm, o_ref, *scratch,
                 l, kk, n_ctx, dec_seq, gate_idx):
    tb, ns = PEER_TOKENS, PEER_SLOTS
    ubufs, vbufs = scratch[:ns], scratch[ns:2 * ns]
    qbuf, abuf, xs, ys, sem_u, sem_v = scratch[2 * ns:]
    nc = xs.shape[0]
    nv = nc // SUBLANES
    n_groups = kk // SUBLANES
    step = pl.program_id(0)
    lane0 = (step % (LANES // tb)) * tb
    gate = mod_ref[_mod_index(step * tb, n_ctx, dec_seq), pl.ds(gate_idx, 1), :]
    sub = lax.broadcasted_iota(jnp.int32, (SUBLANES, LANES), 0)

    def issue_group(hbm, bufs, sem, priority, tn, slot, kg):
        base = tn * kk + kg * SUBLANES
        for j in range(SUBLANES):
            e = idx_ref[base + j]
            pltpu.make_async_copy(hbm.at[l, e], bufs[slot].at[kg * SUBLANES + j],
                                  sem.at[slot]).start(priority=priority)

    issue_u = functools.partial(issue_group, u_hbm, ubufs, sem_u, 0)
    issue_v = functools.partial(issue_group, v_hbm, vbufs, sem_v, 1)

    def wait_slabs(hbm, bufs, sem, slot):
        pltpu.make_async_copy(hbm.at[l, pl.ds(0, kk)], bufs[slot], sem.at[slot]).wait()

    def fold_sublanes(parts):
        dist = SUBLANES // 2
        while len(parts) > 1:
            low = (sub & dist) == 0
            half = len(parts) // 2
            nxt = []
            for i in range(half):
                keep = jnp.where(low, parts[i], parts[i + half])
                move = jnp.where(low, parts[i + half], parts[i])
                up = pltpu.roll(move, SUBLANES - dist, 0)
                swapped = up if 2 * dist == SUBLANES else jnp.where(low, up, pltpu.roll(move, dist, 0))
                nxt.append(keep + swapped)
            parts = nxt
            dist //= 2
        return parts[0]

    def token(t, slot, prefetch):
        tn = t + ns - 1
        nslot = (slot + ns - 1) % ns
        ubuf, vbuf = ubufs[slot], vbufs[slot]
        lane = lax.broadcasted_iota(jnp.int32, (kk, LANES), 1)
        gcol = jnp.sum(jnp.where(lane == lane0 + t, g_ref[...], 0.0), axis=1, keepdims=True)
        xrow = x_ref[pl.ds(t, 1), :]
        for c in range(nc):
            xs[c:c + 1, :] = xrow[:, c * LANES:(c + 1) * LANES]
        xv = [xs[i * SUBLANES:(i + 1) * SUBLANES, :] for i in range(nv)]
        wait_slabs(u_hbm, ubufs, sem_u, slot)

        def u_group(kg, carry):
            if prefetch:
                issue_u(tn, nslot, kg)
            parts = []
            for j in range(SUBLANES):
                uk = ubuf[kg * SUBLANES + j].astype(F32)
                s = uk[0:SUBLANES] * xv[0]
                for i in range(1, nv):
                    s = s + uk[i * SUBLANES:(i + 1) * SUBLANES] * xv[i]
                parts.append(s)
            qbuf[pl.ds(pl.multiple_of(kg * SUBLANES, SUBLANES), SUBLANES), :] = fold_sublanes(parts)
            return carry
        lax.fori_loop(0, n_groups, u_group, 0, unroll=2)

        h = jnp.sum(qbuf[...], axis=-1, keepdims=True)
        a = gcol * (0.5 * h * (1.0 + lax.erf(h * np.float32(0.5 ** 0.5))))
        abuf[...] = jnp.broadcast_to(a, (kk, LANES))
        wait_slabs(v_hbm, vbufs, sem_v, slot)

        def v_group(kg, acc):
            if prefetch:
                issue_v(tn, nslot, kg)
            acc = list(acc)
            for j in range(SUBLANES):
                k = kg * SUBLANES + j
                ak = jnp.broadcast_to(abuf[pl.ds(k, 1), :], (SUBLANES, LANES))
                vk = vbuf[k].astype(F32)
                for i in range(nv):
                    n = (j % 2) * nv + i
                    acc[n] = acc[n] + vk[i * SUBLANES:(i + 1) * SUBLANES] * ak
            return tuple(acc)
        zero = jnp.zeros((SUBLANES, LANES), F32)
        acc = lax.fori_loop(0, n_groups, v_group, (zero,) * (2 * nv), unroll=2)
        for i in range(nv):
            ys[i * SUBLANES:(i + 1) * SUBLANES, :] = acc[i] + acc[nv + i]
        y = jnp.concatenate([ys[c:c + 1, :] for c in range(nc)], axis=1)
        o_ref[pl.ds(t, 1), :] = res_ref[pl.ds(t, 1), :] + gate * y

    def token_round(r, prefetch):
        for slot in range(ns):
            token(r * ns + slot, slot, prefetch[slot])

    for t in range(ns - 1):
        lax.fori_loop(0, n_groups, lambda kg, c, t=t: (issue_u(t, t, kg), issue_v(t, t, kg), c)[2], 0)
    n_rounds = tb // ns
    lax.fori_loop(0, n_rounds - 1, lambda r, c: (token_round(r, (True,) * ns), c)[1], 0)
    token_round(n_rounds - 1, (True,) + (False,) * (ns - 1))


def _peer(idx, g, x, res, mod, u, v, l, gate_idx, n_ctx, dec_seq):
    n, d = x.shape
    kk = idx.shape[1]
    r = mod.shape[1]
    nc = d // LANES
    tb, ns = PEER_TOKENS, PEER_SLOTS
    assert kk % SUBLANES == 0 and LANES % tb == 0 and n % LANES == 0
    assert nc % (2 * SUBLANES) == 0
    kern = functools.partial(_peer_kernel, l=l, kk=kk, n_ctx=n_ctx, dec_seq=dec_seq, gate_idx=gate_idx)
    return pl.pallas_call(
        kern,
        grid=(n // tb,),
        in_specs=[
            pl.BlockSpec((tb * kk,), lambda i: (i,), memory_space=pltpu.SMEM),
            pl.BlockSpec((kk, LANES), lambda i: (0, i // (LANES // tb))),
            pl.BlockSpec((tb, d), lambda i: (i, 0)),
            pl.BlockSpec((tb, d), lambda i: (i, 0)),
            pl.BlockSpec((None, r, 6, d), lambda i: (l, 0, 0, 0)),
            pl.BlockSpec(memory_space=pl.ANY),
            pl.BlockSpec(memory_space=pl.ANY),
        ],
        out_specs=pl.BlockSpec((tb, d), lambda i: (i, 0)),
        out_shape=jax.ShapeDtypeStruct((n, d), F32),
        scratch_shapes=[pltpu.VMEM((kk, nc, LANES), u.dtype) for _ in range(ns)] + [
            pltpu.VMEM((kk, nc, LANES), v.dtype) for _ in range(ns)] + [
            pltpu.VMEM((kk, LANES), F32),
            pltpu.VMEM((kk, LANES), F32),
            pltpu.VMEM((nc, LANES), F32),
            pltpu.VMEM((nc, LANES), F32),
            pltpu.SemaphoreType.DMA((ns,)),
            pltpu.SemaphoreType.DMA((ns,)),
        ],
        compiler_params=_cparams(("arbitrary",)),
        name="peer_experts",
    )(idx.reshape(n * kk), g, x, res, mod, u, v)


def kernel(x_prompt, x_sample, cache_na_k, cache_na_v, cache_swa_k, cache_swa_v, c, c_ctx,
           w_mod, b_mod, norm1_g, w_in, rpb, sink, out_norm_a, out_norm_b, w_out, norm2_g,
           peer_wq, peer_subkeys, peer_u, peer_v, final_g):
    batch, seq, d = x_prompt.shape
    dec_batch, dec_seq, _ = x_sample.shape
    depth = w_mod.shape[0]
    past = cache_na_k.shape[2]
    na_heads = cache_na_k.shape[3]
    swa_kv = cache_swa_k.shape[3]
    swa_heads = sink.shape[1]
    group = swa_heads // swa_kv
    n_ctx = batch * seq
    scale = HEAD_DIM ** -0.5
    qa_col, ka_col, va_col = 0, na_heads, 2 * na_heads
    qb_col = 3 * na_heads
    kb_col = qb_col + swa_heads
    vb_col = kb_col + swa_kv

    x = jnp.concatenate([x_prompt.reshape(n_ctx, d), x_sample.reshape(dec_batch * dec_seq, d)], axis=0)

    n_mod = 1 + dec_batch
    r_mod = -(-n_mod // SUBLANES) * SUBLANES
    cond = jnp.concatenate([c_ctx[None, :], c, jnp.zeros((r_mod - n_mod, d), F32)], axis=0)
    mod = _adaln(cond, w_mod, b_mod).reshape(depth, r_mod, 6, d)

    cos, sin = _rope_tables(dec_seq)
    na_bias = _na_bias(rpb, dec_seq // GRID_W)
    u_tab, v_tab = _expert_tables(peer_u), _expert_tables(peer_v)
    na_k, na_v, swa_k, swa_v = [], [], [], []
    for l in range(depth):
        (h,) = _norm_mod(x, norm1_g, mod, l, 0, 1, n_ctx, dec_seq, [BF16])
        p = _matmul(h, w_in, l, n_ctx, dec_seq)
        oa_ctx = _ctx_attn(p, None, batch=batch, seq=seq, n_kv=na_heads, group=1,
                           q_col=qa_col, k_col=ka_col, v_col=va_col, scale=scale)
        ob_ctx = _ctx_attn(p, sink[l], batch=batch, seq=seq, n_kv=swa_kv, group=group,
                           q_col=qb_col, k_col=kb_col, v_col=vb_col, scale=scale)
        oa_lat = _na_lat(p, cache_na_k, cache_na_v, na_bias, l,
                         n_ctx=n_ctx, batch=dec_batch, t=dec_seq, heads=na_heads, past=past, scale=scale)
        ob_lat = _swa_lat(p, cache_swa_k, cache_swa_v, sink[l], cos, sin, l,
                          n_ctx=n_ctx, batch=dec_batch, t=dec_seq, n_kv=swa_kv, group=group, past=past,
                          q_col=qb_col, k_col=kb_col, v_col=vb_col, scale=scale)
        hn = _group_norm(oa_ctx, ob_ctx, oa_lat, ob_lat, out_norm_a, out_norm_b, l)
        x = _matmul_residual(hn, w_out, x, mod, l, 2, n_ctx, dec_seq)

        h2_bf, h2 = _norm_mod(x, norm2_g, mod, l, 3, 4, n_ctx, dec_seq, [BF16, F32])
        q = _matmul(h2_bf, peer_wq, l, n_ctx, dec_seq)
        idx_t, g_t = _peer_topk(q, peer_subkeys, l)
        x = _peer(idx_t.T, g_t, h2, x, mod, u_tab, v_tab, l, 5, n_ctx, dec_seq)

        pc = p[:n_ctx]
        na_k.append(pc[:, ka_col * HEAD_DIM:va_col * HEAD_DIM].reshape(batch, seq, na_heads, HEAD_DIM))
        na_v.append(pc[:, va_col * HEAD_DIM:qb_col * HEAD_DIM].reshape(batch, seq, na_heads, HEAD_DIM))
        swa_k.append(pc[:, kb_col * HEAD_DIM:vb_col * HEAD_DIM].reshape(batch, seq, swa_kv, HEAD_DIM))
        swa_v.append(pc[:, vb_col * HEAD_DIM:].reshape(batch, seq, swa_kv, HEAD_DIM))

    y = _final_norm(x, final_g)
    return (y[:n_ctx].reshape(batch, seq, d), y[n_ctx:].reshape(dec_batch, dec_seq, d),
            jnp.stack(na_k, axis=1), jnp.stack(na_v, axis=1),
            jnp.stack(swa_k, axis=1), jnp.stack(swa_v, axis=1))
```

```python
import functools

import jax
import jax.numpy as jnp
import numpy as np
from jax import lax
from jax.experimental import pallas as pl
from jax.experimental.pallas import tpu as pltpu

GRID_W = 64
NA_ROWS = 8
NA_COLS = 16
SWA_WINDOW = 128
SWA_BLOCK = 128
ROPE_BASE = 10000.0
PEER_TOPK = 16
EPS = 1e-6
NEG_INF = -1e30

HEAD_DIM = 128
LANES = 128
SUBLANES = 8
VMEM_LIMIT = 56 * 1024 * 1024

F32 = jnp.float32
BF16 = jnp.bfloat16


def _cparams(sem):
    return pltpu.CompilerParams(dimension_semantics=sem, vmem_limit_bytes=VMEM_LIMIT)


def _mod_index(row0, n_ctx, dec_seq):
    return jnp.where(row0 < n_ctx, 0, 1 + (row0 - n_ctx) // dec_seq)


def _adaln_kernel(c_ref, w_ref, b_ref, o_ref):
    c = c_ref[...]
    s = (c * jax.nn.sigmoid(c)).astype(BF16)
    o_ref[...] = jnp.dot(s, w_ref[...].astype(BF16), preferred_element_type=F32) + b_ref[...]


def _adaln(cond, w_mod, b_mod):
    depth, d, n = w_mod.shape
    r = cond.shape[0]
    tn = next(t for t in (512, 256, LANES) if n % t == 0)
    return pl.pallas_call(
        _adaln_kernel,
        grid=(depth, n // tn),
        in_specs=[
            pl.BlockSpec((r, d), lambda l, j: (0, 0)),
            pl.BlockSpec((None, d, tn), lambda l, j: (l, 0, j)),
            pl.BlockSpec((None, 1, tn), lambda l, j: (l, 0, j)),
        ],
        out_specs=pl.BlockSpec((None, r, tn), lambda l, j: (l, 0, j)),
        out_shape=jax.ShapeDtypeStruct((depth, r, n), F32),
        compiler_params=_cparams(("arbitrary", "arbitrary")),
        name="adaln",
    )(cond, w_mod, b_mod.reshape(depth, 1, n))


def _norm_mod_kernel(x_ref, g_ref, mod_ref, *o_refs, tr, n_ctx, dec_seq, shift_idx, scale_idx):
    x = x_ref[...]
    y = x * lax.rsqrt(jnp.mean(x * x, axis=-1, keepdims=True) + EPS) * g_ref[...]
    m = _mod_index(pl.program_id(0) * tr, n_ctx, dec_seq)
    shift = mod_ref[m, pl.ds(shift_idx, 1), :]
    scale = mod_ref[m, pl.ds(scale_idx, 1), :]
    h = y * (1.0 + scale) + shift
    for o_ref in o_refs:
        o_ref[...] = h.astype(o_ref.dtype)


def _norm_mod(x, g, mod, l, shift_idx, scale_idx, n_ctx, dec_seq, out_dtypes):
    n, d = x.shape
    tr = 256
    depth, r = mod.shape[:2]
    kern = functools.partial(_norm_mod_kernel, tr=tr, n_ctx=n_ctx, dec_seq=dec_seq,
                             shift_idx=shift_idx, scale_idx=scale_idx)
    outs = pl.pallas_call(
        kern,
        grid=(n // tr,),
        in_specs=[
            pl.BlockSpec((tr, d), lambda i: (i, 0)),
            pl.BlockSpec((None, 1, d), lambda i: (l, 0, 0)),
            pl.BlockSpec((None, r, 6, d), lambda i: (l, 0, 0, 0)),
        ],
        out_specs=[pl.BlockSpec((tr, d), lambda i: (i, 0)) for _ in out_dtypes],
        out_shape=[jax.ShapeDtypeStruct((n, d), dt) for dt in out_dtypes],
        compiler_params=_cparams(("arbitrary",)),
        name="norm_mod",
    )(x, g.reshape(depth, 1, d), mod)
    return outs


def _final_norm_kernel(x_ref, g_ref, o_ref):
    x = x_ref[...]
    o_ref[...] = x * lax.rsqrt(jnp.mean(x * x, axis=-1, keepdims=True) + EPS) * g_ref[...]


def _final_norm(x, g):
    n, d = x.shape
    tr = 256
    return pl.pallas_call(
        _final_norm_kernel,
        grid=(n // tr,),
        in_specs=[pl.BlockSpec((tr, d), lambda i: (i, 0)), pl.BlockSpec((1, d), lambda i: (0, 0))],
        out_specs=pl.BlockSpec((tr, d), lambda i: (i, 0)),
        out_shape=jax.ShapeDtypeStruct((n, d), F32),
        compiler_params=_cparams(("arbitrary",)),
        name="final_norm",
    )(x, g.reshape(1, d))


def _group_norm_kernel(oac_ref, obc_ref, oal_ref, obl_ref, ga_ref, gb_ref, h_ref, *, mix_a, ctx_tiles):
    def nrm(x, g):
        return (x * lax.rsqrt(jnp.mean(x * x, axis=-1, keepdims=True) + EPS) * g).astype(h_ref.dtype)

    def emit(oa_ref, ob_ref):
        h_ref[:, :mix_a] = nrm(oa_ref[...], ga_ref[...])
        h_ref[:, mix_a:] = nrm(ob_ref[...], gb_ref[...])

    is_ctx = pl.program_id(0) < ctx_tiles
    pl.when(is_ctx)(lambda: emit(oac_ref, obc_ref))
    pl.when(jnp.logical_not(is_ctx))(lambda: emit(oal_ref, obl_ref))


def _group_norm(oa_ctx, ob_ctx, oa_lat, ob_lat, ga, gb, l):
    n_ctx, mix_a = oa_ctx.shape
    n_lat, mix_b = ob_lat.shape
    depth = ga.shape[0]
    tr = 256
    ctx_tiles = n_ctx // tr
    lat_tiles = n_lat // tr
    ctx_map = lambda i: (jnp.minimum(i, ctx_tiles - 1), 0)
    lat_map = lambda i: (jnp.maximum(i - ctx_tiles, 0), 0)
    return pl.pallas_call(
        functools.partial(_group_norm_kernel, mix_a=mix_a, ctx_tiles=ctx_tiles),
        grid=(ctx_tiles + lat_tiles,),
        in_specs=[
            pl.BlockSpec((tr, mix_a), ctx_map),
            pl.BlockSpec((tr, mix_b), ctx_map),
            pl.BlockSpec((tr, mix_a), lat_map),
            pl.BlockSpec((tr, mix_b), lat_map),
            pl.BlockSpec((None, 1, mix_a), lambda i: (l, 0, 0)),
            pl.BlockSpec((None, 1, mix_b), lambda i: (l, 0, 0)),
        ],
        out_specs=pl.BlockSpec((tr, mix_a + mix_b), lambda i: (i, 0)),
        out_shape=jax.ShapeDtypeStruct((n_ctx + n_lat, mix_a + mix_b), BF16),
        compiler_params=_cparams(("arbitrary",)),
        name="group_norm",
    )(oa_ctx, ob_ctx, oa_lat, ob_lat, ga.reshape(depth, 1, mix_a), gb.reshape(depth, 1, mix_b))


def _mm_kernel(a_ref, w_ref, o_ref):
    o_ref[...] = jnp.dot(a_ref[...], w_ref[...].astype(BF16), preferred_element_type=F32)


def _mm_res_kernel(a_ref, w_ref, x_ref, mod_ref, o_ref, *, tm, n_ctx, dec_seq, gate_idx):
    acc = jnp.dot(a_ref[...], w_ref[...].astype(BF16), preferred_element_type=F32)
    m = _mod_index(pl.program_id(0) * tm, n_ctx, dec_seq)
    gate = mod_ref[m, pl.ds(gate_idx, 1), :]
    o_ref[...] = x_ref[...] + gate * acc


def _row_tile(n_ctx, dec_seq):
    tm = min(1024, dec_seq)
    assert n_ctx % tm == 0 and dec_seq % tm == 0
    return tm


def _col_tile(m):
    return next(t for t in (512, 256, LANES) if m % t == 0)


def _matmul(a, w, l, n_ctx, dec_seq):
    n, k = a.shape
    m = w.shape[2]
    tm = _row_tile(n_ctx, dec_seq)
    tn = _col_tile(m)
    return pl.pallas_call(
        _mm_kernel,
        grid=(n // tm, m // tn),
        in_specs=[
            pl.BlockSpec((tm, k), lambda i, j: (i, 0)),
            pl.BlockSpec((None, k, tn), lambda i, j: (l, 0, j)),
        ],
        out_specs=pl.BlockSpec((tm, tn), lambda i, j: (i, j)),
        out_shape=jax.ShapeDtypeStruct((n, m), F32),
        compiler_params=_cparams(("arbitrary", "arbitrary")),
        name="matmul",
    )(a, w)


def _matmul_residual(a, w, x, mod, l, gate_idx, n_ctx, dec_seq):
    n, k = a.shape
    m = w.shape[2]
    r = mod.shape[1]
    tm = _row_tile(n_ctx, dec_seq)
    tn = _col_tile(m)
    kern = functools.partial(_mm_res_kernel, tm=tm, n_ctx=n_ctx, dec_seq=dec_seq, gate_idx=gate_idx)
    return pl.pallas_call(
        kern,
        grid=(n // tm, m // tn),
        in_specs=[
            pl.BlockSpec((tm, k), lambda i, j: (i, 0)),
            pl.BlockSpec((None, k, tn), lambda i, j: (l, 0, j)),
            pl.BlockSpec((tm, tn), lambda i, j: (i, j)),
            pl.BlockSpec((None, r, 6, tn), lambda i, j: (l, 0, 0, j)),
        ],
        out_specs=pl.BlockSpec((tm, tn), lambda i, j: (i, j)),
        out_shape=jax.ShapeDtypeStruct((n, m), F32),
        compiler_params=_cparams(("arbitrary", "arbitrary")),
        name="matmul_residual",
    )(a, w, x, mod)


_NT = (((1,), (1,)), ((), ()))


def _qk(q, k):
    return lax.dot_general(q, k, _NT, preferred_element_type=F32)


def _pv(p, v):
    return jnp.dot(p.astype(BF16), v, preferred_element_type=F32)


def _softmax_pv(score_blocks, value_blocks, sink=None):
    m = functools.reduce(jnp.maximum, [jnp.max(s, axis=-1, keepdims=True) for s in score_blocks])
    if sink is not None:
        m = jnp.maximum(m, sink)
    ps = [jnp.exp(s - m) for s in score_blocks]
    denom = functools.reduce(jnp.add, [jnp.sum(p, axis=-1, keepdims=True) for p in ps])
    if sink is not None:
        denom = denom + jnp.exp(sink - m)
    o = functools.reduce(jnp.add, [_pv(p, v) for p, v in zip(ps, value_blocks)])
    return o / denom


def _ctx_attn_kernel(*refs, group, scale, has_sink):
    if has_sink:
        sink_ref, q_ref, k_ref, v_ref, o_ref = refs
    else:
        q_ref, k_ref, v_ref, o_ref = refs
    k = k_ref[...].astype(BF16)
    v = v_ref[...].astype(BF16)
    kv = pl.program_id(1)
    for g in range(group):
        q = q_ref[:, g * HEAD_DIM:(g + 1) * HEAD_DIM].astype(BF16)
        s = _qk(q, k) * scale
        sink = sink_ref[kv * group + g] if has_sink else None
        o_ref[:, g * HEAD_DIM:(g + 1) * HEAD_DIM] = _softmax_pv([s], [v], sink)


def _ctx_attn(p, sink_l, *, batch, seq, n_kv, group, q_col, k_col, v_col, scale):
    gw = group * HEAD_DIM
    has_sink = sink_l is not None
    kern = functools.partial(_ctx_attn_kernel, group=group, scale=scale, has_sink=has_sink)
    in_specs = [
        pl.BlockSpec((seq, gw), lambda b, h: (b, q_col // group + h)),
        pl.BlockSpec((seq, HEAD_DIM), lambda b, h: (b, k_col + h)),
        pl.BlockSpec((seq, HEAD_DIM), lambda b, h: (b, v_col + h)),
    ]
    args = [p, p, p]
    if has_sink:
        in_specs = [pl.BlockSpec(memory_space=pltpu.SMEM)] + in_specs
        args = [sink_l] + args
    return pl.pallas_call(
        kern,
        grid=(batch, n_kv),
        in_specs=in_specs,
        out_specs=pl.BlockSpec((seq, gw), lambda b, h: (b, h)),
        out_shape=jax.ShapeDtypeStruct((batch * seq, n_kv * gw), F32),
        compiler_params=_cparams(("arbitrary", "arbitrary")),
        name="ctx_attn_sink" if has_sink else "ctx_attn",
    )(*args)


def _na_groups(rows):
    kr = min(NA_ROWS, rows)
    starts = np.clip(np.arange(rows) - kr // 2, 0, rows - kr)
    groups, r = [], 0
    while r < rows:
        e = r
        while e + 1 < rows and starts[e + 1] == starts[r]:
            e += 1
        groups.append((r, e - r + 1, int(starts[r])))
        r = e + 1
    return kr, groups


def _na_bias(rpb, rows):
    depth, heads = rpb.shape[:2]
    kr = min(NA_ROWS, rows)
    r = np.arange(rows)
    d0 = np.clip(r - kr // 2, 0, rows - kr) - r + (NA_ROWS - 1)
    by_row = jnp.stack([rpb[:, :, int(s):int(s) + kr, :] for s in d0], axis=2)
    cq = np.arange(GRID_W)
    cs = np.clip(cq - NA_COLS // 2, 0, GRID_W - NA_COLS)
    col_ok = (cq[None, :] >= cs[:, None]) & (cq[None, :] < cs[:, None] + NA_COLS)
    dc = np.clip(cq[None, :] - cq[:, None] + (NA_COLS - 1), 0, 2 * NA_COLS - 2)
    onehot = (dc[None, :, :] == np.arange(2 * NA_COLS - 1)[:, None, None]).astype(np.float32)
    bias = jnp.einsum('lhrij,jqk->lhrqik', by_row.astype(F32), onehot, precision=lax.Precision.HIGHEST)
    bias = jnp.where(col_ok[None, None, None, :, None, :], bias, NEG_INF)
    return bias.reshape(depth, heads, rows * GRID_W, kr * GRID_W)


def _na_lat_kernel(q_ref, k_ref, v_ref, kc_ref, vc_ref, bias_ref, o_ref, *, groups, kr, scale):
    q = q_ref[...].astype(BF16)
    k = k_ref[...].astype(BF16)
    v = v_ref[...].astype(BF16)
    kc = kc_ref[...].astype(BF16)
    vc = vc_ref[...].astype(BF16)
    for r0, nr, kr0 in groups:
        q0, q1 = r0 * GRID_W, (r0 + nr) * GRID_W
        k0, k1 = kr0 * GRID_W, (kr0 + kr) * GRID_W
        qg = q[q0:q1]
        s_lat = _qk(qg, k[k0:k1]) * scale + bias_ref[q0:q1, :]
        s_ctx = _qk(qg, kc) * scale
        o_ref[q0:q1, :] = _softmax_pv([s_lat, s_ctx], [v[k0:k1], vc])


def _na_lat(p, cache_k, cache_v, bias, l, *, n_ctx, batch, t, heads, past, scale):
    rows = t // GRID_W
    kr, groups = _na_groups(rows)
    rb = n_ctx // t
    depth = cache_k.shape[1]
    ck = cache_k.reshape(batch, depth, past, heads * HEAD_DIM)
    cv = cache_v.reshape(batch, depth, past, heads * HEAD_DIM)
    kern = functools.partial(_na_lat_kernel, groups=groups, kr=kr, scale=scale)
    return pl.pallas_call(
        kern,
        grid=(heads, batch),
        in_specs=[
            pl.BlockSpec((t, HEAD_DIM), lambda h, b: (rb + b, h)),
            pl.BlockSpec((t, HEAD_DIM), lambda h, b: (rb + b, heads + h)),
            pl.BlockSpec((t, HEAD_DIM), lambda h, b: (rb + b, 2 * heads + h)),
            pl.BlockSpec((None, None, past, HEAD_DIM), lambda h, b: (b, l, 0, h)),
            pl.BlockSpec((None, None, past, HEAD_DIM), lambda h, b: (b, l, 0, h)),
            pl.BlockSpec((None, None, t, kr * GRID_W), lambda h, b: (l, h, 0, 0)),
        ],
        out_specs=pl.BlockSpec((t, HEAD_DIM), lambda h, b: (b, h)),
        out_shape=jax.ShapeDtypeStruct((batch * t, heads * HEAD_DIM), F32),
        compiler_params=_cparams(("arbitrary", "arbitrary")),
        name="na_latent",
    )(p, p, p, ck, cv, bias)


def _rope_tables(t):
    pos = np.arange(t)
    nf = HEAD_DIM // 4
    inv = jnp.asarray(ROPE_BASE, F32) ** (-jnp.arange(nf, dtype=F32) / nf)
    ang_r = jnp.asarray(pos // GRID_W, F32)[:, None] * inv
    ang_c = jnp.asarray(pos % GRID_W, F32)[:, None] * inv
    cos = jnp.concatenate([jnp.cos(ang_r)] * 2 + [jnp.cos(ang_c)] * 2, axis=-1)
    sin = jnp.concatenate([-jnp.sin(ang_r), jnp.sin(ang_r), -jnp.sin(ang_c), jnp.sin(ang_c)], axis=-1)
    return cos, sin


def _rope(x, cos, sin):
    nf = HEAD_DIM // 4
    lane = lax.broadcasted_iota(jnp.int32, x.shape, 1)
    swapped = jnp.where(lane % (2 * nf) < nf,
                        pltpu.roll(x, HEAD_DIM - nf, 1),
                        pltpu.roll(x, nf, 1))
    return x * cos + swapped * sin


def _swa_lat_kernel(sink_ref, q_ref, k_ref, v_ref, kc_ref, vc_ref, cos_ref, sin_ref, o_ref,
                    krot_ref, vbf_ref, *, group, t, scale):
    kv = pl.program_id(1)
    krot_ref[...] = _rope(k_ref[...], cos_ref[...], sin_ref[...]).astype(BF16)
    vbf_ref[...] = v_ref[...].astype(BF16)
    kc = kc_ref[...].astype(BF16)
    vc = vc_ref[...].astype(BF16)
    span = 3 * SWA_BLOCK

    def block(n, carry):
        q0 = pl.multiple_of(n * SWA_BLOCK, SWA_BLOCK)
        w0 = pl.multiple_of(jnp.clip((n - 1) * SWA_BLOCK, 0, t - span), SWA_BLOCK)
        kw = krot_ref[pl.ds(w0, span), :]
        vw = vbf_ref[pl.ds(w0, span), :]
        qpos = q0 + lax.broadcasted_iota(jnp.int32, (SWA_BLOCK, span), 0)
        kpos = w0 + lax.broadcasted_iota(jnp.int32, (SWA_BLOCK, span), 1)
        valid = jnp.abs(kpos - qpos) <= SWA_WINDOW
        cos = cos_ref[pl.ds(q0, SWA_BLOCK), :]
        sin = sin_ref[pl.ds(q0, SWA_BLOCK), :]
        for g in range(group):
            cols = slice(g * HEAD_DIM, (g + 1) * HEAD_DIM)
            qg = _rope(q_ref[pl.ds(q0, SWA_BLOCK), cols], cos, sin).astype(BF16)
            s_lat = jnp.where(valid, _qk(qg, kw) * scale, NEG_INF)
            s_ctx = _qk(qg, kc) * scale
            o_ref[pl.ds(q0, SWA_BLOCK), cols] = _softmax_pv([s_lat, s_ctx], [vw, vc],
                                                           sink_ref[kv * group + g])
        return carry

    lax.fori_loop(0, t // SWA_BLOCK, block, 0)


def _swa_lat(p, cache_k, cache_v, sink_l, cos, sin, l, *, n_ctx, batch, t, n_kv, group, past,
             q_col, k_col, v_col, scale):
    assert t >= 3 * SWA_BLOCK and t % SWA_BLOCK == 0
    gw = group * HEAD_DIM
    rb = n_ctx // t
    depth = cache_k.shape[1]
    ck = cache_k.reshape(batch, depth, past, n_kv * HEAD_DIM)
    cv = cache_v.reshape(batch, depth, past, n_kv * HEAD_DIM)
    kern = functools.partial(_swa_lat_kernel, group=group, t=t, scale=scale)
    return pl.pallas_call(
        kern,
        grid=(batch, n_kv),
        in_specs=[
            pl.BlockSpec(memory_space=pltpu.SMEM),
            pl.BlockSpec((t, gw), lambda b, h: (rb + b, q_col // group + h)),
            pl.BlockSpec((t, HEAD_DIM), lambda b, h: (rb + b, k_col + h)),
            pl.BlockSpec((t, HEAD_DIM), lambda b, h: (rb + b, v_col + h)),
            pl.BlockSpec((None, None, past, HEAD_DIM), lambda b, h: (b, l, 0, h)),
            pl.BlockSpec((None, None, past, HEAD_DIM), lambda b, h: (b, l, 0, h)),
            pl.BlockSpec((t, HEAD_DIM), lambda b, h: (0, 0)),
            pl.BlockSpec((t, HEAD_DIM), lambda b, h: (0, 0)),
        ],
        out_specs=pl.BlockSpec((t, gw), lambda b, h: (b, h)),
        out_shape=jax.ShapeDtypeStruct((batch * t, n_kv * gw), F32),
        scratch_shapes=[pltpu.VMEM((t, HEAD_DIM), BF16), pltpu.VMEM((t, HEAD_DIM), BF16)],
        compiler_params=_cparams(("arbitrary", "arbitrary")),
        name="swa_latent",
    )(sink_l, p, p, p, ck, cv, cos, sin)


def _topk_rows(v, k):
    n = v.shape[0]
    iota = lax.broadcasted_iota(jnp.int32, v.shape, 0)
    vals, idxs = [], []
    for _ in range(k):
        m = jnp.max(v, axis=0, keepdims=True)
        first = jnp.min(jnp.where(v == m, iota, n), axis=0, keepdims=True)
        v = jnp.where(iota == first, -jnp.inf, v)
        vals.append(m)
        idxs.append(first)
    return jnp.concatenate(vals, axis=0), jnp.concatenate(idxs, axis=0)


def _peer_topk_kernel(q_ref, sk_ref, idx_ref, g_ref, *, n_keys, key_dim):
    k = PEER_TOPK
    sv, si = [], []
    for part in range(2):
        q = q_ref[:, part * key_dim:(part + 1) * key_dim].astype(BF16)
        s = _qk(sk_ref[part].astype(BF16), q)
        v, i = _topk_rows(s, k)
        sv.append(v)
        si.append(i)
    cand = jnp.concatenate([sv[0][a:a + 1] + sv[1] for a in range(k)], axis=0)
    cv, cp = _topk_rows(cand, k)
    ca, cb = cp // k, cp % k
    e1 = jnp.zeros_like(cp)
    e2 = jnp.zeros_like(cp)
    for a in range(k):
        e1 = jnp.where(ca == a, si[0][a:a + 1], e1)
        e2 = jnp.where(cb == a, si[1][a:a + 1], e2)
    idx_ref[...] = e1 * n_keys + e2
    e = jnp.exp(cv - jnp.max(cv, axis=0, keepdims=True))
    g_ref[...] = e / jnp.sum(e, axis=0, keepdims=True)


def _peer_topk(q, subkeys, l):
    n = q.shape[0]
    depth, heads, _, n_keys, key_dim = subkeys.shape
    tq = 256
    k = PEER_TOPK
    kern = functools.partial(_peer_topk_kernel, n_keys=n_keys, key_dim=key_dim)
    idx, g = pl.pallas_call(
        kern,
        grid=(n // tq, heads),
        in_specs=[
            pl.BlockSpec((tq, 2 * key_dim), lambda i, h: (i, h)),
            pl.BlockSpec((None, None, 2, n_keys, key_dim), lambda i, h: (l, h, 0, 0, 0)),
        ],
        out_specs=[pl.BlockSpec((None, k, tq), lambda i, h: (h, 0, i))] * 2,
        out_shape=[jax.ShapeDtypeStruct((heads, k, n), jnp.int32),
                   jax.ShapeDtypeStruct((heads, k, n), F32)],
        compiler_params=_cparams(("arbitrary", "arbitrary")),
        name="peer_topk",
    )(q, subkeys)
    return idx.reshape(heads * k, n), g.reshape(heads * k, n)


PEER_TOKENS = 64
PEER_SLOTS = 4
PEER_DTYPE = BF16


def _expert_table(u, v):
    depth, e, d = u.shape
    nc = d // LANES
    uv = jnp.concatenate([u.astype(PEER_DTYPE).reshape(depth, e, nc, LANES),
                          v.astype(PEER_DTYPE).reshape(depth, e, nc, LANES)], axis=2)
    return uv


def _peer_kernel(idx_ref, g_ref, x_ref, res_ref, mod_ref, uv_hbm, o_ref, *scratch,
                 l, kk, n_ctx, dec_seq, gate_idx):
    tb, ns = PEER_TOKENS, PEER_SLOTS
    bufs = scratch[:ns]
    hbuf, abuf, xs, ys, sem = scratch[ns:]
    nc = xs.shape[0]
    nv = nc // SUBLANES
    n_groups = kk // SUBLANES
    step = pl.program_id(0)
    gate = mod_ref[_mod_index(step * tb, n_ctx, dec_seq), pl.ds(gate_idx, 1), :]
    sub = lax.broadcasted_iota(jnp.int32, (SUBLANES, LANES), 0)
    lane = lax.broadcasted_iota(jnp.int32, (SUBLANES, LANES), 1)

    def issue_group(t, slot, kg, js=range(SUBLANES)):
        base = t * kk + kg * SUBLANES
        for j in js:
            e = idx_ref[base + j]
            pltpu.make_async_copy(uv_hbm.at[l, e], bufs[slot].at[kg * SUBLANES + j],
                                  sem.at[slot]).start(priority=j % 2)

    def wait_slot(slot):
        pltpu.make_async_copy(uv_hbm.at[l, pl.ds(0, kk)], bufs[slot], sem.at[slot]).wait()

    def fold_sublanes(parts):
        dist = SUBLANES // 2
        while len(parts) > 1:
            low = (sub & dist) == 0
            half = len(parts) // 2
            nxt = []
            for i in range(half):
                keep = jnp.where(low, parts[i], parts[i + half])
                move = jnp.where(low, parts[i + half], parts[i])
                up = pltpu.roll(move, SUBLANES - dist, 0)
                swapped = up if 2 * dist == SUBLANES else jnp.where(low, up, pltpu.roll(move, dist, 0))
                nxt.append(keep + swapped)
            parts = nxt
            dist //= 2
        return parts[0]

    def pre_activations(t, slot, fetch):
        buf = bufs[slot]
        xrow = x_ref[pl.ds(t, 1), :]
        for c in range(nc):
            xs[c:c + 1, :] = xrow[:, c * LANES:(c + 1) * LANES]
        xv = [xs[i * SUBLANES:(i + 1) * SUBLANES, :] for i in range(nv)]
        hmat = jnp.zeros((SUBLANES, LANES), F32)
        for kg in range(n_groups):
            fetch(kg)
            parts = []
            for j in range(SUBLANES):
                uk = buf[kg * SUBLANES + j, 0:nc, :].astype(F32)
                s = uk[0:SUBLANES] * xv[0]
                for i in range(1, nv):
                    s = s + uk[i * SUBLANES:(i + 1) * SUBLANES] * xv[i]
                parts.append(s)
            hk = jnp.sum(fold_sublanes(parts), axis=-1, keepdims=True)
            hmat = jnp.where(lane == kg, hk, hmat)
        hbuf[...] = hmat

    def activations(t):
        hmat = hbuf[...]
        amat = g_ref[t] * (0.5 * hmat * (1.0 + lax.erf(hmat * np.float32(0.5 ** 0.5))))
        for kg in range(n_groups):
            abuf[kg * SUBLANES:(kg + 1) * SUBLANES, :] = jnp.broadcast_to(amat[:, kg:kg + 1], (SUBLANES, LANES))

    def combine(t, slot, fetch):
        buf = bufs[slot]
        acc = [jnp.zeros((SUBLANES, LANES), F32)] * (2 * nv)
        for k in range(kk):
            if k % SUBLANES == 0:
                fetch(k // SUBLANES)
            ak = jnp.broadcast_to(abuf[k:k + 1, :], (SUBLANES, LANES))
            vk = buf[k, nc:2 * nc, :].astype(F32)
            for i in range(nv):
                n = (k % 2) * nv + i
                acc[n] = acc[n] + vk[i * SUBLANES:(i + 1) * SUBLANES] * ak
        for i in range(nv):
            ys[i * SUBLANES:(i + 1) * SUBLANES, :] = acc[i] + acc[nv + i]
        y = jnp.concatenate([ys[c:c + 1, :] for c in range(nc)], axis=1)
        o_ref[pl.ds(t, 1), :] = res_ref[pl.ds(t, 1), :] + gate * y

    def stage(s, j, do_issue, do_act):
        half = SUBLANES // 2
        nothing = lambda kg: None
        fetch_lo = fetch_hi = nothing
        if do_issue:
            fetch_lo = lambda kg: issue_group(s + ns - 1, (j + ns - 1) % ns, kg, range(half))
            fetch_hi = lambda kg: issue_group(s + ns - 1, (j + ns - 1) % ns, kg, range(half, SUBLANES))
        if do_act:
            wait_slot((j + 1) % ns)
        activations(s)
        if do_act:
            pre_activations(s + 1, (j + 1) % ns, fetch_lo)
        else:
            for kg in range(n_groups):
                fetch_lo(kg)
        combine(s, j, fetch_hi)

    for t in range(ns - 1):
        lax.fori_loop(0, n_groups, lambda kg, c, t=t: (issue_group(t, t, kg), c)[1], 0)
    wait_slot(0)
    pre_activations(0, 0, lambda kg: None)
    n_rounds = tb // ns

    def full_round(r, c):
        for j in range(ns):
            stage(r * ns + j, j, True, True)
        return c
    lax.fori_loop(0, n_rounds - 1, full_round, 0)
    for j in range(ns):
        stage((n_rounds - 1) * ns + j, j, j == 0, j < ns - 1)


def _peer(idx, g, x, res, mod, uv, l, gate_idx, n_ctx, dec_seq):
    n, d = x.shape
    kk = idx.shape[1]
    r = mod.shape[1]
    nc = d // LANES
    tb, ns = PEER_TOKENS, PEER_SLOTS
    n_groups = kk // SUBLANES
    assert kk % SUBLANES == 0 and n_groups <= LANES and n % tb == 0 and tb % ns == 0 and ns % 2 == 0
    assert nc % (2 * SUBLANES) == 0
    g_tiles = jnp.pad(g.reshape(n_groups, SUBLANES, n).transpose(2, 1, 0),
                      ((0, 0), (0, 0), (0, LANES - n_groups)))
    kern = functools.partial(_peer_kernel, l=l, kk=kk, n_ctx=n_ctx, dec_seq=dec_seq, gate_idx=gate_idx)
    return pl.pallas_call(
        kern,
        grid=(n // tb,),
        in_specs=[
            pl.BlockSpec((tb * kk,), lambda i: (i,), memory_space=pltpu.SMEM),
            pl.BlockSpec((tb, SUBLANES, LANES), lambda i: (i, 0, 0)),
            pl.BlockSpec((tb, d), lambda i: (i, 0)),
            pl.BlockSpec((tb, d), lambda i: (i, 0)),
            pl.BlockSpec((None, r, 6, d), lambda i: (l, 0, 0, 0)),
            pl.BlockSpec(memory_space=pl.ANY),
        ],
        out_specs=pl.BlockSpec((tb, d), lambda i: (i, 0)),
        out_shape=jax.ShapeDtypeStruct((n, d), F32),
        scratch_shapes=[pltpu.VMEM((kk, 2 * nc, LANES), uv.dtype) for _ in range(ns)] + [
            pltpu.VMEM((SUBLANES, LANES), F32),
            pltpu.VMEM((kk, LANES), F32),
            pltpu.VMEM((nc, LANES), F32),
            pltpu.VMEM((nc, LANES), F32),
            pltpu.SemaphoreType.DMA((ns,)),
        ],
        compiler_params=_cparams(("arbitrary",)),
        name="peer_experts",
    )(idx.reshape(n * kk), g_tiles, x, res, mod, uv)


def kernel(x_prompt, x_sample, cache_na_k, cache_na_v, cache_swa_k, cache_swa_v, c, c_ctx,
           w_mod, b_mod, norm1_g, w_in, rpb, sink, out_norm_a, out_norm_b, w_out, norm2_g,
           peer_wq, peer_subkeys, peer_u, peer_v, final_g):
    batch, seq, d = x_prompt.shape
    dec_batch, dec_seq, _ = x_sample.shape
    depth = w_mod.shape[0]
    past = cache_na_k.shape[2]
    na_heads = cache_na_k.shape[3]
    swa_kv = cache_swa_k.shape[3]
    swa_heads = sink.shape[1]
    group = swa_heads // swa_kv
    n_ctx = batch * seq
    scale = HEAD_DIM ** -0.5
    qa_col, ka_col, va_col = 0, na_heads, 2 * na_heads
    qb_col = 3 * na_heads
    kb_col = qb_col + swa_heads
    vb_col = kb_col + swa_kv

    x = jnp.concatenate([x_prompt.reshape(n_ctx, d), x_sample.reshape(dec_batch * dec_seq, d)], axis=0)

    n_mod = 1 + dec_batch
    r_mod = -(-n_mod // SUBLANES) * SUBLANES
    cond = jnp.concatenate([c_ctx[None, :], c, jnp.zeros((r_mod - n_mod, d), F32)], axis=0)
    mod = _adaln(cond, w_mod, b_mod).reshape(depth, r_mod, 6, d)

    cos, sin = _rope_tables(dec_seq)
    na_bias = _na_bias(rpb, dec_seq // GRID_W)
    uv_tab = _expert_table(peer_u, peer_v)
    na_k, na_v, swa_k, swa_v = [], [], [], []
    for l in range(depth):
        (h,) = _norm_mod(x, norm1_g, mod, l, 0, 1, n_ctx, dec_seq, [BF16])
        p = _matmul(h, w_in, l, n_ctx, dec_seq)
        oa_ctx = _ctx_attn(p, None, batch=batch, seq=seq, n_kv=na_heads, group=1,
                           q_col=qa_col, k_col=ka_col, v_col=va_col, scale=scale)
        ob_ctx = _ctx_attn(p, sink[l], batch=batch, seq=seq, n_kv=swa_kv, group=group,
                           q_col=qb_col, k_col=kb_col, v_col=vb_col, scale=scale)
        oa_lat = _na_lat(p, cache_na_k, cache_na_v, na_bias, l,
                         n_ctx=n_ctx, batch=dec_batch, t=dec_seq, heads=na_heads, past=past, scale=scale)
        ob_lat = _swa_lat(p, cache_swa_k, cache_swa_v, sink[l], cos, sin, l,
                          n_ctx=n_ctx, batch=dec_batch, t=dec_seq, n_kv=swa_kv, group=group, past=past,
                          q_col=qb_col, k_col=kb_col, v_col=vb_col, scale=scale)
        hn = _group_norm(oa_ctx, ob_ctx, oa_lat, ob_lat, out_norm_a, out_norm_b, l)
        x = _matmul_residual(hn, w_out, x, mod, l, 2, n_ctx, dec_seq)

        h2_bf, h2 = _norm_mod(x, norm2_g, mod, l, 3, 4, n_ctx, dec_seq, [BF16, F32])
        q = _matmul(h2_bf, peer_wq, l, n_ctx, dec_seq)
        idx_t, g_t = _peer_topk(q, peer_subkeys, l)
        x = _peer(idx_t.T, g_t, h2, x, mod, uv_tab, l, 5, n_ctx, dec_seq)

        pc = p[:n_ctx]
        na_k.append(pc[:, ka_col * HEAD_DIM:va_col * HEAD_DIM].reshape(batch, seq, na_heads, HEAD_DIM))
        na_v.append(pc[:, va_col * HEAD_DIM:qb_col * HEAD_DIM].reshape(batch, seq, na_heads, HEAD_DIM))
        swa_k.append(pc[:, kb_col * HEAD_DIM:vb_col * HEAD_DIM].reshape(batch, seq, swa_kv, HEAD_DIM))
        swa_v.append(pc[:, vb_col * HEAD_DIM:].reshape(batch, seq, swa_kv, HEAD_DIM))

    y = _final_norm(x, final_g)
    return (y[:n_ctx].reshape(batch, seq, d), y[n_ctx:].reshape(dec_batch, dec_seq, d),
            jnp.stack(na_k, axis=1), jnp.stack(na_v, axis=1),
            jnp.stack(swa_k, axis=1), jnp.stack(swa_v, axis=1))
```

```python
import functools

import jax
import jax.numpy as jnp
import numpy as np
from jax import lax
from jax.experimental import pallas as pl
from jax.experimental.pallas import tpu as pltpu

GRID_W = 64
NA_ROWS = 8
NA_COLS = 16
SWA_WINDOW = 128
SWA_BLOCK = 128
ROPE_BASE = 10000.0
PEER_TOPK = 16
EPS = 1e-6
NEG_INF = -1e30

HEAD_DIM = 128
LANES = 128
SUBLANES = 8
VMEM_LIMIT = 56 * 1024 * 1024

F32 = jnp.float32
BF16 = jnp.bfloat16


def _cparams(sem):
    return pltpu.CompilerParams(dimension_semantics=sem, vmem_limit_bytes=VMEM_LIMIT)


def _mod_index(row0, n_ctx, dec_seq):
    return jnp.where(row0 < n_ctx, 0, 1 + (row0 - n_ctx) // dec_seq)


def _adaln_kernel(c_ref, w_ref, b_ref, o_ref):
    c = c_ref[...]
    s = (c * jax.nn.sigmoid(c)).astype(BF16)
    o_ref[...] = jnp.dot(s, w_ref[...].astype(BF16), preferred_element_type=F32) + b_ref[...]


def _adaln(cond, w_mod, b_mod):
    depth, d, n = w_mod.shape
    r = cond.shape[0]
    tn = next(t for t in (512, 256, LANES) if n % t == 0)
    return pl.pallas_call(
        _adaln_kernel,
        grid=(depth, n // tn),
        in_specs=[
            pl.BlockSpec((r, d), lambda l, j: (0, 0)),
            pl.BlockSpec((None, d, tn), lambda l, j: (l, 0, j)),
            pl.BlockSpec((None, 1, tn), lambda l, j: (l, 0, j)),
        ],
        out_specs=pl.BlockSpec((None, r, tn), lambda l, j: (l, 0, j)),
        out_shape=jax.ShapeDtypeStruct((depth, r, n), F32),
        compiler_params=_cparams(("arbitrary", "arbitrary")),
        name="adaln",
    )(cond, w_mod, b_mod.reshape(depth, 1, n))


def _norm_mod_kernel(x_ref, g_ref, mod_ref, *o_refs, tr, n_ctx, dec_seq, shift_idx, scale_idx):
    x = x_ref[...]
    y = x * lax.rsqrt(jnp.mean(x * x, axis=-1, keepdims=True) + EPS) * g_ref[...]
    m = _mod_index(pl.program_id(0) * tr, n_ctx, dec_seq)
    shift = mod_ref[m, pl.ds(shift_idx, 1), :]
    scale = mod_ref[m, pl.ds(scale_idx, 1), :]
    h = y * (1.0 + scale) + shift
    for o_ref in o_refs:
        o_ref[...] = h.astype(o_ref.dtype)


def _norm_mod(x, g, mod, l, shift_idx, scale_idx, n_ctx, dec_seq, out_dtypes):
    n, d = x.shape
    tr = 256
    depth, r = mod.shape[:2]
    kern = functools.partial(_norm_mod_kernel, tr=tr, n_ctx=n_ctx, dec_seq=dec_seq,
                             shift_idx=shift_idx, scale_idx=scale_idx)
    outs = pl.pallas_call(
        kern,
        grid=(n // tr,),
        in_specs=[
            pl.BlockSpec((tr, d), lambda i: (i, 0)),
            pl.BlockSpec((None, 1, d), lambda i: (l, 0, 0)),
            pl.BlockSpec((None, r, 6, d), lambda i: (l, 0, 0, 0)),
        ],
        out_specs=[pl.BlockSpec((tr, d), lambda i: (i, 0)) for _ in out_dtypes],
        out_shape=[jax.ShapeDtypeStruct((n, d), dt) for dt in out_dtypes],
        compiler_params=_cparams(("arbitrary",)),
        name="norm_mod",
    )(x, g.reshape(depth, 1, d), mod)
    return outs


def _final_norm_kernel(x_ref, g_ref, o_ref):
    x = x_ref[...]
    o_ref[...] = x * lax.rsqrt(jnp.mean(x * x, axis=-1, keepdims=True) + EPS) * g_ref[...]


def _final_norm(x, g):
    n, d = x.shape
    tr = 256
    return pl.pallas_call(
        _final_norm_kernel,
        grid=(n // tr,),
        in_specs=[pl.BlockSpec((tr, d), lambda i: (i, 0)), pl.BlockSpec((1, d), lambda i: (0, 0))],
        out_specs=pl.BlockSpec((tr, d), lambda i: (i, 0)),
        out_shape=jax.ShapeDtypeStruct((n, d), F32),
        compiler_params=_cparams(("arbitrary",)),
        name="final_norm",
    )(x, g.reshape(1, d))


def _group_norm_kernel(oac_ref, obc_ref, oal_ref, obl_ref, ga_ref, gb_ref, h_ref, *, mix_a, ctx_tiles):
    def nrm(x, g):
        return (x * lax.rsqrt(jnp.mean(x * x, axis=-1, keepdims=True) + EPS) * g).astype(h_ref.dtype)

    def emit(oa_ref, ob_ref):
        h_ref[:, :mix_a] = nrm(oa_ref[...], ga_ref[...])
        h_ref[:, mix_a:] = nrm(ob_ref[...], gb_ref[...])

    is_ctx = pl.program_id(0) < ctx_tiles
    pl.when(is_ctx)(lambda: emit(oac_ref, obc_ref))
    pl.when(jnp.logical_not(is_ctx))(lambda: emit(oal_ref, obl_ref))


def _group_norm(oa_ctx, ob_ctx, oa_lat, ob_lat, ga, gb, l):
    n_ctx, mix_a = oa_ctx.shape
    n_lat, mix_b = ob_lat.shape
    depth = ga.shape[0]
    tr = 256
    ctx_tiles = n_ctx // tr
    lat_tiles = n_lat // tr
    ctx_map = lambda i: (jnp.minimum(i, ctx_tiles - 1), 0)
    lat_map = lambda i: (jnp.maximum(i - ctx_tiles, 0), 0)
    return pl.pallas_call(
        functools.partial(_group_norm_kernel, mix_a=mix_a, ctx_tiles=ctx_tiles),
        grid=(ctx_tiles + lat_tiles,),
        in_specs=[
            pl.BlockSpec((tr, mix_a), ctx_map),
            pl.BlockSpec((tr, mix_b), ctx_map),
            pl.BlockSpec((tr, mix_a), lat_map),
            pl.BlockSpec((tr, mix_b), lat_map),
            pl.BlockSpec((None, 1, mix_a), lambda i: (l, 0, 0)),
            pl.BlockSpec((None, 1, mix_b), lambda i: (l, 0, 0)),
        ],
        out_specs=pl.BlockSpec((tr, mix_a + mix_b), lambda i: (i, 0)),
        out_shape=jax.ShapeDtypeStruct((n_ctx + n_lat, mix_a + mix_b), BF16),
        compiler_params=_cparams(("arbitrary",)),
        name="group_norm",
    )(oa_ctx, ob_ctx, oa_lat, ob_lat, ga.reshape(depth, 1, mix_a), gb.reshape(depth, 1, mix_b))


def _mm_kernel(a_ref, w_ref, o_ref):
    o_ref[...] = jnp.dot(a_ref[...], w_ref[...].astype(BF16), preferred_element_type=F32)


def _mm_res_kernel(a_ref, w_ref, x_ref, mod_ref, o_ref, *, tm, n_ctx, dec_seq, gate_idx):
    acc = jnp.dot(a_ref[...], w_ref[...].astype(BF16), preferred_element_type=F32)
    m = _mod_index(pl.program_id(0) * tm, n_ctx, dec_seq)
    gate = mod_ref[m, pl.ds(gate_idx, 1), :]
    o_ref[...] = x_ref[...] + gate * acc


def _row_tile(n_ctx, dec_seq):
    tm = min(1024, dec_seq)
    assert n_ctx % tm == 0 and dec_seq % tm == 0
    return tm


def _col_tile(m):
    return next(t for t in (512, 256, LANES) if m % t == 0)


def _matmul(a, w, l, n_ctx, dec_seq):
    n, k = a.shape
    m = w.shape[2]
    tm = _row_tile(n_ctx, dec_seq)
    tn = _col_tile(m)
    return pl.pallas_call(
        _mm_kernel,
        grid=(n // tm, m // tn),
        in_specs=[
            pl.BlockSpec((tm, k), lambda i, j: (i, 0)),
            pl.BlockSpec((None, k, tn), lambda i, j: (l, 0, j)),
        ],
        out_specs=pl.BlockSpec((tm, tn), lambda i, j: (i, j)),
        out_shape=jax.ShapeDtypeStruct((n, m), F32),
        compiler_params=_cparams(("arbitrary", "arbitrary")),
        name="matmul",
    )(a, w)


def _matmul_residual(a, w, x, mod, l, gate_idx, n_ctx, dec_seq):
    n, k = a.shape
    m = w.shape[2]
    r = mod.shape[1]
    tm = _row_tile(n_ctx, dec_seq)
    tn = _col_tile(m)
    kern = functools.partial(_mm_res_kernel, tm=tm, n_ctx=n_ctx, dec_seq=dec_seq, gate_idx=gate_idx)
    return pl.pallas_call(
        kern,
        grid=(n // tm, m // tn),
        in_specs=[
            pl.BlockSpec((tm, k), lambda i, j: (i, 0)),
            pl.BlockSpec((None, k, tn), lambda i, j: (l, 0, j)),
            pl.BlockSpec((tm, tn), lambda i, j: (i, j)),
            pl.BlockSpec((None, r, 6, tn), lambda i, j: (l, 0, 0, j)),
        ],
        out_specs=pl.BlockSpec((tm, tn), lambda i, j: (i, j)),
        out_shape=jax.ShapeDtypeStruct((n, m), F32),
        compiler_params=_cparams(("arbitrary", "arbitrary")),
        name="matmul_residual",
    )(a, w, x, mod)


_NT = (((1,), (1,)), ((), ()))


def _qk(q, k):
    return lax.dot_general(q, k, _NT, preferred_element_type=F32)


def _pv(p, v):
    return jnp.dot(p.astype(BF16), v, preferred_element_type=F32)


def _softmax_pv(score_blocks, value_blocks, sink=None):
    m = functools.reduce(jnp.maximum, [jnp.max(s, axis=-1, keepdims=True) for s in score_blocks])
    if sink is not None:
        m = jnp.maximum(m, sink)
    ps = [jnp.exp(s - m) for s in score_blocks]
    denom = functools.reduce(jnp.add, [jnp.sum(p, axis=-1, keepdims=True) for p in ps])
    if sink is not None:
        denom = denom + jnp.exp(sink - m)
    o = functools.reduce(jnp.add, [_pv(p, v) for p, v in zip(ps, value_blocks)])
    return o / denom


def _ctx_attn_kernel(*refs, group, scale, has_sink):
    if has_sink:
        sink_ref, q_ref, k_ref, v_ref, o_ref = refs
    else:
        q_ref, k_ref, v_ref, o_ref = refs
    k = k_ref[...].astype(BF16)
    v = v_ref[...].astype(BF16)
    kv = pl.program_id(1)
    for g in range(group):
        q = q_ref[:, g * HEAD_DIM:(g + 1) * HEAD_DIM].astype(BF16)
        s = _qk(q, k) * scale
        sink = sink_ref[kv * group + g] if has_sink else None
        o_ref[:, g * HEAD_DIM:(g + 1) * HEAD_DIM] = _softmax_pv([s], [v], sink)


def _ctx_attn(p, sink_l, *, batch, seq, n_kv, group, q_col, k_col, v_col, scale):
    gw = group * HEAD_DIM
    has_sink = sink_l is not None
    kern = functools.partial(_ctx_attn_kernel, group=group, scale=scale, has_sink=has_sink)
    in_specs = [
        pl.BlockSpec((seq, gw), lambda b, h: (b, q_col // group + h)),
        pl.BlockSpec((seq, HEAD_DIM), lambda b, h: (b, k_col + h)),
        pl.BlockSpec((seq, HEAD_DIM), lambda b, h: (b, v_col + h)),
    ]
    args = [p, p, p]
    if has_sink:
        in_specs = [pl.BlockSpec(memory_space=pltpu.SMEM)] + in_specs
        args = [sink_l] + args
    return pl.pallas_call(
        kern,
        grid=(batch, n_kv),
        in_specs=in_specs,
        out_specs=pl.BlockSpec((seq, gw), lambda b, h: (b, h)),
        out_shape=jax.ShapeDtypeStruct((batch * seq, n_kv * gw), F32),
        compiler_params=_cparams(("arbitrary", "arbitrary")),
        name="ctx_attn_sink" if has_sink else "ctx_attn",
    )(*args)


def _na_groups(rows):
    kr = min(NA_ROWS, rows)
    starts = np.clip(np.arange(rows) - kr // 2, 0, rows - kr)
    groups, r = [], 0
    while r < rows:
        e = r
        while e + 1 < rows and starts[e + 1] == starts[r]:
            e += 1
        groups.append((r, e - r + 1, int(starts[r])))
        r = e + 1
    return kr, groups


def _na_bias(rpb, rows):
    depth, heads = rpb.shape[:2]
    kr = min(NA_ROWS, rows)
    r = np.arange(rows)
    d0 = np.clip(r - kr // 2, 0, rows - kr) - r + (NA_ROWS - 1)
    by_row = jnp.stack([rpb[:, :, int(s):int(s) + kr, :] for s in d0], axis=2)
    cq = np.arange(GRID_W)
    cs = np.clip(cq - NA_COLS // 2, 0, GRID_W - NA_COLS)
    col_ok = (cq[None, :] >= cs[:, None]) & (cq[None, :] < cs[:, None] + NA_COLS)
    dc = np.clip(cq[None, :] - cq[:, None] + (NA_COLS - 1), 0, 2 * NA_COLS - 2)
    onehot = (dc[None, :, :] == np.arange(2 * NA_COLS - 1)[:, None, None]).astype(np.float32)
    bias = jnp.einsum('lhrij,jqk->lhrqik', by_row.astype(F32), onehot, precision=lax.Precision.HIGHEST)
    bias = jnp.where(col_ok[None, None, None, :, None, :], bias, NEG_INF)
    return bias.reshape(depth, heads, rows * GRID_W, kr * GRID_W)


def _na_lat_kernel(q_ref, k_ref, v_ref, kc_ref, vc_ref, bias_ref, o_ref, *, groups, kr, scale):
    q = q_ref[...].astype(BF16)
    k = k_ref[...].astype(BF16)
    v = v_ref[...].astype(BF16)
    kc = kc_ref[...].astype(BF16)
    vc = vc_ref[...].astype(BF16)
    for r0, nr, kr0 in groups:
        q0, q1 = r0 * GRID_W, (r0 + nr) * GRID_W
        k0, k1 = kr0 * GRID_W, (kr0 + kr) * GRID_W
        qg = q[q0:q1]
        s_lat = _qk(qg, k[k0:k1]) * scale + bias_ref[q0:q1, :]
        s_ctx = _qk(qg, kc) * scale
        o_ref[q0:q1, :] = _softmax_pv([s_lat, s_ctx], [v[k0:k1], vc])


def _na_lat(p, cache_k, cache_v, bias, l, *, n_ctx, batch, t, heads, past, scale):
    rows = t // GRID_W
    kr, groups = _na_groups(rows)
    rb = n_ctx // t
    depth = cache_k.shape[1]
    ck = cache_k.reshape(batch, depth, past, heads * HEAD_DIM)
    cv = cache_v.reshape(batch, depth, past, heads * HEAD_DIM)
    kern = functools.partial(_na_lat_kernel, groups=groups, kr=kr, scale=scale)
    return pl.pallas_call(
        kern,
        grid=(heads, batch),
        in_specs=[
            pl.BlockSpec((t, HEAD_DIM), lambda h, b: (rb + b, h)),
            pl.BlockSpec((t, HEAD_DIM), lambda h, b: (rb + b, heads + h)),
            pl.BlockSpec((t, HEAD_DIM), lambda h, b: (rb + b, 2 * heads + h)),
            pl.BlockSpec((None, None, past, HEAD_DIM), lambda h, b: (b, l, 0, h)),
            pl.BlockSpec((None, None, past, HEAD_DIM), lambda h, b: (b, l, 0, h)),
            pl.BlockSpec((None, None, t, kr * GRID_W), lambda h, b: (l, h, 0, 0)),
        ],
        out_specs=pl.BlockSpec((t, HEAD_DIM), lambda h, b: (b, h)),
        out_shape=jax.ShapeDtypeStruct((batch * t, heads * HEAD_DIM), F32),
        compiler_params=_cparams(("arbitrary", "arbitrary")),
        name="na_latent",
    )(p, p, p, ck, cv, bias)


def _rope_tables(t):
    pos = np.arange(t)
    nf = HEAD_DIM // 4
    inv = jnp.asarray(ROPE_BASE, F32) ** (-jnp.arange(nf, dtype=F32) / nf)
    ang_r = jnp.asarray(pos // GRID_W, F32)[:, None] * inv
    ang_c = jnp.asarray(pos % GRID_W, F32)[:, None] * inv
    cos = jnp.concatenate([jnp.cos(ang_r)] * 2 + [jnp.cos(ang_c)] * 2, axis=-1)
    sin = jnp.concatenate([-jnp.sin(ang_r), jnp.sin(ang_r), -jnp.sin(ang_c), jnp.sin(ang_c)], axis=-1)
    return cos, sin


def _rope(x, cos, sin):
    nf = HEAD_DIM // 4
    lane = lax.broadcasted_iota(jnp.int32, x.shape, 1)
    swapped = jnp.where(lane % (2 * nf) < nf,
                        pltpu.roll(x, HEAD_DIM - nf, 1),
                        pltpu.roll(x, nf, 1))
    return x * cos + swapped * sin


def _swa_lat_kernel(sink_ref, q_ref, k_ref, v_ref, kc_ref, vc_ref, cos_ref, sin_ref, o_ref,
                    krot_ref, vbf_ref, *, group, t, scale):
    kv = pl.program_id(1)
    krot_ref[...] = _rope(k_ref[...], cos_ref[...], sin_ref[...]).astype(BF16)
    vbf_ref[...] = v_ref[...].astype(BF16)
    kc = kc_ref[...].astype(BF16)
    vc = vc_ref[...].astype(BF16)
    span = 3 * SWA_BLOCK

    def block(n, carry):
        q0 = pl.multiple_of(n * SWA_BLOCK, SWA_BLOCK)
        w0 = pl.multiple_of(jnp.clip((n - 1) * SWA_BLOCK, 0, t - span), SWA_BLOCK)
        kw = krot_ref[pl.ds(w0, span), :]
        vw = vbf_ref[pl.ds(w0, span), :]
        qpos = q0 + lax.broadcasted_iota(jnp.int32, (SWA_BLOCK, span), 0)
        kpos = w0 + lax.broadcasted_iota(jnp.int32, (SWA_BLOCK, span), 1)
        valid = jnp.abs(kpos - qpos) <= SWA_WINDOW
        cos = cos_ref[pl.ds(q0, SWA_BLOCK), :]
        sin = sin_ref[pl.ds(q0, SWA_BLOCK), :]
        for g in range(group):
            cols = slice(g * HEAD_DIM, (g + 1) * HEAD_DIM)
            qg = _rope(q_ref[pl.ds(q0, SWA_BLOCK), cols], cos, sin).astype(BF16)
            s_lat = jnp.where(valid, _qk(qg, kw) * scale, NEG_INF)
            s_ctx = _qk(qg, kc) * scale
            o_ref[pl.ds(q0, SWA_BLOCK), cols] = _softmax_pv([s_lat, s_ctx], [vw, vc],
                                                           sink_ref[kv * group + g])
        return carry

    lax.fori_loop(0, t // SWA_BLOCK, block, 0)


def _swa_lat(p, cache_k, cache_v, sink_l, cos, sin, l, *, n_ctx, batch, t, n_kv, group, past,
             q_col, k_col, v_col, scale):
    assert t >= 3 * SWA_BLOCK and t % SWA_BLOCK == 0
    gw = group * HEAD_DIM
    rb = n_ctx // t
    depth = cache_k.shape[1]
    ck = cache_k.reshape(batch, depth, past, n_kv * HEAD_DIM)
    cv = cache_v.reshape(batch, depth, past, n_kv * HEAD_DIM)
    kern = functools.partial(_swa_lat_kernel, group=group, t=t, scale=scale)
    return pl.pallas_call(
        kern,
        grid=(batch, n_kv),
        in_specs=[
            pl.BlockSpec(memory_space=pltpu.SMEM),
            pl.BlockSpec((t, gw), lambda b, h: (rb + b, q_col // group + h)),
            pl.BlockSpec((t, HEAD_DIM), lambda b, h: (rb + b, k_col + h)),
            pl.BlockSpec((t, HEAD_DIM), lambda b, h: (rb + b, v_col + h)),
            pl.BlockSpec((None, None, past, HEAD_DIM), lambda b, h: (b, l, 0, h)),
            pl.BlockSpec((None, None, past, HEAD_DIM), lambda b, h: (b, l, 0, h)),
            pl.BlockSpec((t, HEAD_DIM), lambda b, h: (0, 0)),
            pl.BlockSpec((t, HEAD_DIM), lambda b, h: (0, 0)),
        ],
        out_specs=pl.BlockSpec((t, gw), lambda b, h: (b, h)),
        out_shape=jax.ShapeDtypeStruct((batch * t, n_kv * gw), F32),
        scratch_shapes=[pltpu.VMEM((t, HEAD_DIM), BF16), pltpu.VMEM((t, HEAD_DIM), BF16)],
        compiler_params=_cparams(("arbitrary", "arbitrary")),
        name="swa_latent",
    )(sink_l, p, p, p, ck, cv, cos, sin)


def _topk_rows(v, k):
    n = v.shape[0]
    iota = lax.broadcasted_iota(jnp.int32, v.shape, 0)
    vals, idxs = [], []
    for _ in range(k):
        m = jnp.max(v, axis=0, keepdims=True)
        first = jnp.min(jnp.where(v == m, iota, n), axis=0, keepdims=True)
        v = jnp.where(iota == first, -jnp.inf, v)
        vals.append(m)
        idxs.append(first)
    return jnp.concatenate(vals, axis=0), jnp.concatenate(idxs, axis=0)


def _peer_topk_kernel(q_ref, sk_ref, idx_ref, g_ref, *, n_keys, key_dim):
    k = PEER_TOPK
    sv, si = [], []
    for part in range(2):
        q = q_ref[:, part * key_dim:(part + 1) * key_dim].astype(BF16)
        s = _qk(sk_ref[part].astype(BF16), q)
        v, i = _topk_rows(s, k)
        sv.append(v)
        si.append(i)
    cand = jnp.concatenate([sv[0][a:a + 1] + sv[1] for a in range(k)], axis=0)
    cv, cp = _topk_rows(cand, k)
    ca, cb = cp // k, cp % k
    e1 = jnp.zeros_like(cp)
    e2 = jnp.zeros_like(cp)
    for a in range(k):
        e1 = jnp.where(ca == a, si[0][a:a + 1], e1)
        e2 = jnp.where(cb == a, si[1][a:a + 1], e2)
    idx_ref[...] = e1 * n_keys + e2
    e = jnp.exp(cv - jnp.max(cv, axis=0, keepdims=True))
    g_ref[...] = e / jnp.sum(e, axis=0, keepdims=True)


def _peer_topk(q, subkeys, l):
    n = q.shape[0]
    depth, heads, _, n_keys, key_dim = subkeys.shape
    tq = 256
    k = PEER_TOPK
    kern = functools.partial(_peer_topk_kernel, n_keys=n_keys, key_dim=key_dim)
    idx, g = pl.pallas_call(
        kern,
        grid=(n // tq, heads),
        in_specs=[
            pl.BlockSpec((tq, 2 * key_dim), lambda i, h: (i, h)),
            pl.BlockSpec((None, None, 2, n_keys, key_dim), lambda i, h: (l, h, 0, 0, 0)),
        ],
        out_specs=[pl.BlockSpec((None, k, tq), lambda i, h: (h, 0, i))] * 2,
        out_shape=[jax.ShapeDtypeStruct((heads, k, n), jnp.int32),
                   jax.ShapeDtypeStruct((heads, k, n), F32)],
        compiler_params=_cparams(("arbitrary", "arbitrary")),
        name="peer_topk",
    )(q, subkeys)
    return idx.reshape(heads * k, n), g.reshape(heads * k, n)


PEER_TOKENS = 64
PEER_SLOTS = 8
PEER_DTYPE = BF16


def _expert_table(u, v):
    depth, e, d = u.shape
    nc = d // LANES
    uv = jnp.concatenate([u.astype(PEER_DTYPE).reshape(depth, e, nc, LANES),
                          v.astype(PEER_DTYPE).reshape(depth, e, nc, LANES)], axis=2)
    return uv


def _peer_kernel(idx_ref, g_ref, x_ref, res_ref, mod_ref, uv_hbm, o_ref, *scratch,
                 l, kk, n_ctx, dec_seq, gate_idx):
    tb, ns = PEER_TOKENS, PEER_SLOTS
    bufs = scratch[:ns]
    hbuf, abuf, xs, ys, sem = scratch[ns:]
    nc = xs.shape[0]
    nv = nc // SUBLANES
    n_groups = kk // SUBLANES
    step = pl.program_id(0)
    gate = mod_ref[_mod_index(step * tb, n_ctx, dec_seq), pl.ds(gate_idx, 1), :]
    sub = lax.broadcasted_iota(jnp.int32, (SUBLANES, LANES), 0)
    lane = lax.broadcasted_iota(jnp.int32, (SUBLANES, LANES), 1)

    def issue_group(t, slot, kg, js=range(SUBLANES)):
        base = t * kk + kg * SUBLANES
        for j in js:
            e = idx_ref[base + j]
            pltpu.make_async_copy(uv_hbm.at[l, e], bufs[slot].at[kg * SUBLANES + j],
                                  sem.at[slot]).start(priority=j % 2)

    def wait_slot(slot):
        pltpu.make_async_copy(uv_hbm.at[l, pl.ds(0, kk)], bufs[slot], sem.at[slot]).wait()

    def fold_sublanes(parts):
        dist = SUBLANES // 2
        while len(parts) > 1:
            low = (sub & dist) == 0
            half = len(parts) // 2
            nxt = []
            for i in range(half):
                keep = jnp.where(low, parts[i], parts[i + half])
                move = jnp.where(low, parts[i + half], parts[i])
                up = pltpu.roll(move, SUBLANES - dist, 0)
                swapped = up if 2 * dist == SUBLANES else jnp.where(low, up, pltpu.roll(move, dist, 0))
                nxt.append(keep + swapped)
            parts = nxt
            dist //= 2
        return parts[0]

    def pre_activations(t, slot, fetch):
        buf = bufs[slot]
        xrow = x_ref[pl.ds(t, 1), :]
        for c in range(nc):
            xs[c:c + 1, :] = xrow[:, c * LANES:(c + 1) * LANES]
        xv = [xs[i * SUBLANES:(i + 1) * SUBLANES, :] for i in range(nv)]
        hmat = jnp.zeros((SUBLANES, LANES), F32)
        for kg in range(n_groups):
            fetch(kg)
            parts = []
            for j in range(SUBLANES):
                uk = buf[kg * SUBLANES + j, 0:nc, :].astype(F32)
                s = uk[0:SUBLANES] * xv[0]
                for i in range(1, nv):
                    s = s + uk[i * SUBLANES:(i + 1) * SUBLANES] * xv[i]
                parts.append(s)
            hk = jnp.sum(fold_sublanes(parts), axis=-1, keepdims=True)
            hmat = jnp.where(lane == kg, hk, hmat)
        hbuf[...] = hmat

    def activations(t):
        hmat = hbuf[...]
        amat = g_ref[t] * (0.5 * hmat * (1.0 + lax.erf(hmat * np.float32(0.5 ** 0.5))))
        for kg in range(n_groups):
            abuf[kg * SUBLANES:(kg + 1) * SUBLANES, :] = jnp.broadcast_to(amat[:, kg:kg + 1], (SUBLANES, LANES))

    def combine(t, slot, fetch):
        buf = bufs[slot]
        acc = [jnp.zeros((SUBLANES, LANES), F32)] * (2 * nv)
        for k in range(kk):
            if k % SUBLANES == 0:
                fetch(k // SUBLANES)
            ak = jnp.broadcast_to(abuf[k:k + 1, :], (SUBLANES, LANES))
            vk = buf[k, nc:2 * nc, :].astype(F32)
            for i in range(nv):
                n = (k % 2) * nv + i
                acc[n] = acc[n] + vk[i * SUBLANES:(i + 1) * SUBLANES] * ak
        for i in range(nv):
            ys[i * SUBLANES:(i + 1) * SUBLANES, :] = acc[i] + acc[nv + i]
        y = jnp.concatenate([ys[c:c + 1, :] for c in range(nc)], axis=1)
        o_ref[pl.ds(t, 1), :] = res_ref[pl.ds(t, 1), :] + gate * y

    def stage(s, j, do_issue, do_act):
        half = SUBLANES // 2
        nothing = lambda kg: None
        fetch_lo = fetch_hi = nothing
        if do_issue:
            fetch_lo = lambda kg: issue_group(s + ns - 1, (j + ns - 1) % ns, kg, range(half))
            fetch_hi = lambda kg: issue_group(s + ns - 1, (j + ns - 1) % ns, kg, range(half, SUBLANES))
        if do_act:
            wait_slot((j + 1) % ns)
        activations(s)
        if do_act:
            pre_activations(s + 1, (j + 1) % ns, fetch_lo)
        else:
            for kg in range(n_groups):
                fetch_lo(kg)
        combine(s, j, fetch_hi)

    for t in range(ns - 1):
        lax.fori_loop(0, n_groups, lambda kg, c, t=t: (issue_group(t, t, kg), c)[1], 0)
    wait_slot(0)
    pre_activations(0, 0, lambda kg: None)
    n_rounds = tb // ns

    def full_round(r, c):
        for j in range(ns):
            stage(r * ns + j, j, True, True)
        return c
    lax.fori_loop(0, n_rounds - 1, full_round, 0)
    for j in range(ns):
        stage((n_rounds - 1) * ns + j, j, j == 0, j < ns - 1)


def _peer(idx, g, x, res, mod, uv, l, gate_idx, n_ctx, dec_seq):
    n, d = x.shape
    kk = idx.shape[1]
    r = mod.shape[1]
    nc = d // LANES
    tb, ns = PEER_TOKENS, PEER_SLOTS
    n_groups = kk // SUBLANES
    assert kk % SUBLANES == 0 and n_groups <= LANES and n % tb == 0 and tb % ns == 0 and ns % 2 == 0
    assert nc % (2 * SUBLANES) == 0
    g_tiles = jnp.pad(g.reshape(n_groups, SUBLANES, n).transpose(2, 1, 0),
                      ((0, 0), (0, 0), (0, LANES - n_groups)))
    kern = functools.partial(_peer_kernel, l=l, kk=kk, n_ctx=n_ctx, dec_seq=dec_seq, gate_idx=gate_idx)
    return pl.pallas_call(
        kern,
        grid=(n // tb,),
        in_specs=[
            pl.BlockSpec((tb * kk,), lambda i: (i,), memory_space=pltpu.SMEM),
            pl.BlockSpec((tb, SUBLANES, LANES), lambda i: (i, 0, 0)),
            pl.BlockSpec((tb, d), lambda i: (i, 0)),
            pl.BlockSpec((tb, d), lambda i: (i, 0)),
            pl.BlockSpec((None, r, 6, d), lambda i: (l, 0, 0, 0)),
            pl.BlockSpec(memory_space=pl.ANY),
        ],
        out_specs=pl.BlockSpec((tb, d), lambda i: (i, 0)),
        out_shape=jax.ShapeDtypeStruct((n, d), F32),
        scratch_shapes=[pltpu.VMEM((kk, 2 * nc, LANES), uv.dtype) for _ in range(ns)] + [
            pltpu.VMEM((SUBLANES, LANES), F32),
            pltpu.VMEM((kk, LANES), F32),
            pltpu.VMEM((nc, LANES), F32),
            pltpu.VMEM((nc, LANES), F32),
            pltpu.SemaphoreType.DMA((ns,)),
        ],
        compiler_params=_cparams(("arbitrary",)),
        name="peer_experts",
    )(idx.reshape(n * kk), g_tiles, x, res, mod, uv)


def kernel(x_prompt, x_sample, cache_na_k, cache_na_v, cache_swa_k, cache_swa_v, c, c_ctx,
           w_mod, b_mod, norm1_g, w_in, rpb, sink, out_norm_a, out_norm_b, w_out, norm2_g,
           peer_wq, peer_subkeys, peer_u, peer_v, final_g):
    batch, seq, d = x_prompt.shape
    dec_batch, dec_seq, _ = x_sample.shape
    depth = w_mod.shape[0]
    past = cache_na_k.shape[2]
    na_heads = cache_na_k.shape[3]
    swa_kv = cache_swa_k.shape[3]
    swa_heads = sink.shape[1]
    group = swa_heads // swa_kv
    n_ctx = batch * seq
    scale = HEAD_DIM ** -0.5
    qa_col, ka_col, va_col = 0, na_heads, 2 * na_heads
    qb_col = 3 * na_heads
    kb_col = qb_col + swa_heads
    vb_col = kb_col + swa_kv

    x = jnp.concatenate([x_prompt.reshape(n_ctx, d), x_sample.reshape(dec_batch * dec_seq, d)], axis=0)

    n_mod = 1 + dec_batch
    r_mod = -(-n_mod // SUBLANES) * SUBLANES
    cond = jnp.concatenate([c_ctx[None, :], c, jnp.zeros((r_mod - n_mod, d), F32)], axis=0)
    mod = _adaln(cond, w_mod, b_mod).reshape(depth, r_mod, 6, d)

    cos, sin = _rope_tables(dec_seq)
    na_bias = _na_bias(rpb, dec_seq // GRID_W)
    uv_tab = _expert_table(peer_u, peer_v)
    na_k, na_v, swa_k, swa_v = [], [], [], []
    for l in range(depth):
        (h,) = _norm_mod(x, norm1_g, mod, l, 0, 1, n_ctx, dec_seq, [BF16])
        p = _matmul(h, w_in, l, n_ctx, dec_seq)
        oa_ctx = _ctx_attn(p, None, batch=batch, seq=seq, n_kv=na_heads, group=1,
                           q_col=qa_col, k_col=ka_col, v_col=va_col, scale=scale)
        ob_ctx = _ctx_attn(p, sink[l], batch=batch, seq=seq, n_kv=swa_kv, group=group,
                           q_col=qb_col, k_col=kb_col, v_col=vb_col, scale=scale)
        oa_lat = _na_lat(p, cache_na_k, cache_na_v, na_bias, l,
                         n_ctx=n_ctx, batch=dec_batch, t=dec_seq, heads=na_heads, past=past, scale=scale)
        ob_lat = _swa_lat(p, cache_swa_k, cache_swa_v, sink[l], cos, sin, l,
                          n_ctx=n_ctx, batch=dec_batch, t=dec_seq, n_kv=swa_kv, group=group, past=past,
                          q_col=qb_col, k_col=kb_col, v_col=vb_col, scale=scale)
        hn = _group_norm(oa_ctx, ob_ctx, oa_lat, ob_lat, out_norm_a, out_norm_b, l)
        x = _matmul_residual(hn, w_out, x, mod, l, 2, n_ctx, dec_seq)

        h2_bf, h2 = _norm_mod(x, norm2_g, mod, l, 3, 4, n_ctx, dec_seq, [BF16, F32])
        q = _matmul(h2_bf, peer_wq, l, n_ctx, dec_seq)
        idx_t, g_t = _peer_topk(q, peer_subkeys, l)
        x = _peer(idx_t.T, g_t, h2, x, mod, uv_tab, l, 5, n_ctx, dec_seq)

        pc = p[:n_ctx]
        na_k.append(pc[:, ka_col * HEAD_DIM:va_col * HEAD_DIM].reshape(batch, seq, na_heads, HEAD_DIM))
        na_v.append(pc[:, va_col * HEAD_DIM:qb_col * HEAD_DIM].reshape(batch, seq, na_heads, HEAD_DIM))
        swa_k.append(pc[:, kb_col * HEAD_DIM:vb_col * HEAD_DIM].reshape(batch, seq, swa_kv, HEAD_DIM))
        swa_v.append(pc[:, vb_col * HEAD_DIM:].reshape(batch, seq, swa_kv, HEAD_DIM))

    y = _final_norm(x, final_g)
    return (y[:n_ctx].reshape(batch, seq, d), y[n_ctx:].reshape(dec_batch, dec_seq, d),
            jnp.stack(na_k, axis=1), jnp.stack(na_v, axis=1),
            jnp.stack(swa_k, axis=1), jnp.stack(swa_v, axis=1))
```

```python
import functools

import jax
import jax.numpy as jnp
import numpy as np
from jax import lax
from jax.experimental import pallas as pl
from jax.experimental.pallas import tpu as pltpu

GRID_W = 64
NA_ROWS = 8
NA_COLS = 16
SWA_WINDOW = 128
SWA_BLOCK = 128
ROPE_BASE = 10000.0
PEER_TOPK = 16
EPS = 1e-6
NEG_INF = -1e30

HEAD_DIM = 128
LANES = 128
SUBLANES = 8
VMEM_LIMIT = 56 * 1024 * 1024

F32 = jnp.float32
BF16 = jnp.bfloat16


def _cparams(sem):
    return pltpu.CompilerParams(dimension_semantics=sem, vmem_limit_bytes=VMEM_LIMIT)


def _mod_index(row0, n_ctx, dec_seq):
    return jnp.where(row0 < n_ctx, 0, 1 + (row0 - n_ctx) // dec_seq)


def _adaln_kernel(c_ref, w_ref, b_ref, o_ref):
    c = c_ref[...]
    s = (c * jax.nn.sigmoid(c)).astype(BF16)
    o_ref[...] = jnp.dot(s, w_ref[...].astype(BF16), preferred_element_type=F32) + b_ref[...]


def _adaln(cond, w_mod, b_mod):
    depth, d, n = w_mod.shape
    r = cond.shape[0]
    tn = next(t for t in (1024, 512, 256, LANES) if n % t == 0)
    return pl.pallas_call(
        _adaln_kernel,
        grid=(depth, n // tn),
        in_specs=[
            pl.BlockSpec((r, d), lambda l, j: (0, 0)),
            pl.BlockSpec((None, d, tn), lambda l, j: (l, 0, j)),
            pl.BlockSpec((None, 1, tn), lambda l, j: (l, 0, j)),
        ],
        out_specs=pl.BlockSpec((None, r, tn), lambda l, j: (l, 0, j)),
        out_shape=jax.ShapeDtypeStruct((depth, r, n), F32),
        compiler_params=_cparams(("arbitrary", "arbitrary")),
        name="adaln",
    )(cond, w_mod, b_mod.reshape(depth, 1, n))


def _norm_mod_kernel(x_ref, g_ref, mod_ref, *o_refs, tr, n_ctx, dec_seq, shift_idx, scale_idx):
    x = x_ref[...]
    y = x * lax.rsqrt(jnp.mean(x * x, axis=-1, keepdims=True) + EPS) * g_ref[...]
    m = _mod_index(pl.program_id(0) * tr, n_ctx, dec_seq)
    shift = mod_ref[m, pl.ds(shift_idx, 1), :]
    scale = mod_ref[m, pl.ds(scale_idx, 1), :]
    h = y * (1.0 + scale) + shift
    for o_ref in o_refs:
        o_ref[...] = h.astype(o_ref.dtype)


def _norm_mod(x, g, mod, l, shift_idx, scale_idx, n_ctx, dec_seq, out_dtypes):
    n, d = x.shape
    tr = 256
    depth, r = mod.shape[:2]
    kern = functools.partial(_norm_mod_kernel, tr=tr, n_ctx=n_ctx, dec_seq=dec_seq,
                             shift_idx=shift_idx, scale_idx=scale_idx)
    outs = pl.pallas_call(
        kern,
        grid=(n // tr,),
        in_specs=[
            pl.BlockSpec((tr, d), lambda i: (i, 0)),
            pl.BlockSpec((None, 1, d), lambda i: (l, 0, 0)),
            pl.BlockSpec((None, r, 6, d), lambda i: (l, 0, 0, 0)),
        ],
        out_specs=[pl.BlockSpec((tr, d), lambda i: (i, 0)) for _ in out_dtypes],
        out_shape=[jax.ShapeDtypeStruct((n, d), dt) for dt in out_dtypes],
        compiler_params=_cparams(("arbitrary",)),
        name="norm_mod",
    )(x, g.reshape(depth, 1, d), mod)
    return outs


def _final_norm_kernel(x_ref, g_ref, o_ref):
    x = x_ref[...]
    o_ref[...] = x * lax.rsqrt(jnp.mean(x * x, axis=-1, keepdims=True) + EPS) * g_ref[...]


def _final_norm(x, g):
    n, d = x.shape
    tr = 256
    return pl.pallas_call(
        _final_norm_kernel,
        grid=(n // tr,),
        in_specs=[pl.BlockSpec((tr, d), lambda i: (i, 0)), pl.BlockSpec((1, d), lambda i: (0, 0))],
        out_specs=pl.BlockSpec((tr, d), lambda i: (i, 0)),
        out_shape=jax.ShapeDtypeStruct((n, d), F32),
        compiler_params=_cparams(("arbitrary",)),
        name="final_norm",
    )(x, g.reshape(1, d))


def _group_norm_kernel(oac_ref, obc_ref, oal_ref, obl_ref, ga_ref, gb_ref, h_ref, *, mix_a, ctx_tiles):
    def nrm(x, g):
        return (x * lax.rsqrt(jnp.mean(x * x, axis=-1, keepdims=True) + EPS) * g).astype(h_ref.dtype)

    def emit(oa_ref, ob_ref):
        h_ref[:, :mix_a] = nrm(oa_ref[...], ga_ref[...])
        h_ref[:, mix_a:] = nrm(ob_ref[...], gb_ref[...])

    is_ctx = pl.program_id(0) < ctx_tiles
    pl.when(is_ctx)(lambda: emit(oac_ref, obc_ref))
    pl.when(jnp.logical_not(is_ctx))(lambda: emit(oal_ref, obl_ref))


def _group_norm(oa_ctx, ob_ctx, oa_lat, ob_lat, ga, gb, l):
    n_ctx, mix_a = oa_ctx.shape
    n_lat, mix_b = ob_lat.shape
    depth = ga.shape[0]
    tr = 256
    ctx_tiles = n_ctx // tr
    lat_tiles = n_lat // tr
    ctx_map = lambda i: (jnp.minimum(i, ctx_tiles - 1), 0)
    lat_map = lambda i: (jnp.maximum(i - ctx_tiles, 0), 0)
    return pl.pallas_call(
        functools.partial(_group_norm_kernel, mix_a=mix_a, ctx_tiles=ctx_tiles),
        grid=(ctx_tiles + lat_tiles,),
        in_specs=[
            pl.BlockSpec((tr, mix_a), ctx_map),
            pl.BlockSpec((tr, mix_b), ctx_map),
            pl.BlockSpec((tr, mix_a), lat_map),
            pl.BlockSpec((tr, mix_b), lat_map),
            pl.BlockSpec((None, 1, mix_a), lambda i: (l, 0, 0)),
            pl.BlockSpec((None, 1, mix_b), lambda i: (l, 0, 0)),
        ],
        out_specs=pl.BlockSpec((tr, mix_a + mix_b), lambda i: (i, 0)),
        out_shape=jax.ShapeDtypeStruct((n_ctx + n_lat, mix_a + mix_b), BF16),
        compiler_params=_cparams(("arbitrary",)),
        name="group_norm",
    )(oa_ctx, ob_ctx, oa_lat, ob_lat, ga.reshape(depth, 1, mix_a), gb.reshape(depth, 1, mix_b))


def _mm_kernel(a_ref, w_ref, o_ref):
    o_ref[...] = jnp.dot(a_ref[...], w_ref[...].astype(BF16), preferred_element_type=F32)


def _mm_res_kernel(a_ref, w_ref, x_ref, mod_ref, o_ref, *, tm, n_ctx, dec_seq, gate_idx):
    acc = jnp.dot(a_ref[...], w_ref[...].astype(BF16), preferred_element_type=F32)
    m = _mod_index(pl.program_id(0) * tm, n_ctx, dec_seq)
    gate = mod_ref[m, pl.ds(gate_idx, 1), :]
    o_ref[...] = x_ref[...] + gate * acc


def _row_tile(n_ctx, dec_seq):
    tm = min(1024, dec_seq)
    assert n_ctx % tm == 0 and dec_seq % tm == 0
    return tm


def _col_tile(m):
    return next(t for t in (512, 256, LANES) if m % t == 0)


def _matmul(a, w, l, n_ctx, dec_seq):
    n, k = a.shape
    m = w.shape[2]
    tm = _row_tile(n_ctx, dec_seq)
    tn = _col_tile(m)
    return pl.pallas_call(
        _mm_kernel,
        grid=(n // tm, m // tn),
        in_specs=[
            pl.BlockSpec((tm, k), lambda i, j: (i, 0)),
            pl.BlockSpec((None, k, tn), lambda i, j: (l, 0, j)),
        ],
        out_specs=pl.BlockSpec((tm, tn), lambda i, j: (i, j)),
        out_shape=jax.ShapeDtypeStruct((n, m), F32),
        compiler_params=_cparams(("arbitrary", "arbitrary")),
        name="matmul",
    )(a, w)


def _matmul_residual(a, w, x, mod, l, gate_idx, n_ctx, dec_seq):
    n, k = a.shape
    m = w.shape[2]
    r = mod.shape[1]
    tm = _row_tile(n_ctx, dec_seq)
    tn = _col_tile(m)
    kern = functools.partial(_mm_res_kernel, tm=tm, n_ctx=n_ctx, dec_seq=dec_seq, gate_idx=gate_idx)
    return pl.pallas_call(
        kern,
        grid=(n // tm, m // tn),
        in_specs=[
            pl.BlockSpec((tm, k), lambda i, j: (i, 0)),
            pl.BlockSpec((None, k, tn), lambda i, j: (l, 0, j)),
            pl.BlockSpec((tm, tn), lambda i, j: (i, j)),
            pl.BlockSpec((None, r, 6, tn), lambda i, j: (l, 0, 0, j)),
        ],
        out_specs=pl.BlockSpec((tm, tn), lambda i, j: (i, j)),
        out_shape=jax.ShapeDtypeStruct((n, m), F32),
        compiler_params=_cparams(("arbitrary", "arbitrary")),
        name="matmul_residual",
    )(a, w, x, mod)


_NT = (((1,), (1,)), ((), ()))


def _qk(q, k):
    return lax.dot_general(q, k, _NT, preferred_element_type=F32)


def _pv(p, v):
    return jnp.dot(p.astype(BF16), v, preferred_element_type=F32)


def _softmax_pv(score_blocks, value_blocks, sink=None):
    m = functools.reduce(jnp.maximum, [jnp.max(s, axis=-1, keepdims=True) for s in score_blocks])
    if sink is not None:
        m = jnp.maximum(m, sink)
    ps = [jnp.exp(s - m) for s in score_blocks]
    denom = functools.reduce(jnp.add, [jnp.sum(p, axis=-1, keepdims=True) for p in ps])
    if sink is not None:
        denom = denom + jnp.exp(sink - m)
    o = functools.reduce(jnp.add, [_pv(p, v) for p, v in zip(ps, value_blocks)])
    return o / denom


def _ctx_attn_kernel(*refs, group, scale, has_sink):
    if has_sink:
        sink_ref, q_ref, k_ref, v_ref, o_ref = refs
    else:
        q_ref, k_ref, v_ref, o_ref = refs
    k = k_ref[...].astype(BF16)
    v = v_ref[...].astype(BF16)
    kv = pl.program_id(1)
    for g in range(group):
        q = q_ref[:, g * HEAD_DIM:(g + 1) * HEAD_DIM].astype(BF16)
        s = _qk(q, k) * scale
        sink = sink_ref[kv * group + g] if has_sink else None
        o_ref[:, g * HEAD_DIM:(g + 1) * HEAD_DIM] = _softmax_pv([s], [v], sink)


def _ctx_attn(p, sink_l, *, batch, seq, n_kv, group, q_col, k_col, v_col, scale):
    gw = group * HEAD_DIM
    has_sink = sink_l is not None
    kern = functools.partial(_ctx_attn_kernel, group=group, scale=scale, has_sink=has_sink)
    in_specs = [
        pl.BlockSpec((seq, gw), lambda b, h: (b, q_col // group + h)),
        pl.BlockSpec((seq, HEAD_DIM), lambda b, h: (b, k_col + h)),
        pl.BlockSpec((seq, HEAD_DIM), lambda b, h: (b, v_col + h)),
    ]
    args = [p, p, p]
    if has_sink:
        in_specs = [pl.BlockSpec(memory_space=pltpu.SMEM)] + in_specs
        args = [sink_l] + args
    return pl.pallas_call(
        kern,
        grid=(batch, n_kv),
        in_specs=in_specs,
        out_specs=pl.BlockSpec((seq, gw), lambda b, h: (b, h)),
        out_shape=jax.ShapeDtypeStruct((batch * seq, n_kv * gw), F32),
        compiler_params=_cparams(("arbitrary", "arbitrary")),
        name="ctx_attn_sink" if has_sink else "ctx_attn",
    )(*args)


def _na_groups(rows):
    kr = min(NA_ROWS, rows)
    starts = np.clip(np.arange(rows) - kr // 2, 0, rows - kr)
    groups, r = [], 0
    while r < rows:
        e = r
        while e + 1 < rows and starts[e + 1] == starts[r]:
            e += 1
        groups.append((r, e - r + 1, int(starts[r])))
        r = e + 1
    return kr, groups


def _na_bias(rpb, rows):
    depth, heads = rpb.shape[:2]
    kr = min(NA_ROWS, rows)
    r = np.arange(rows)
    d0 = np.clip(r - kr // 2, 0, rows - kr) - r + (NA_ROWS - 1)
    by_row = jnp.stack([rpb[:, :, int(s):int(s) + kr, :] for s in d0], axis=2)
    cq = np.arange(GRID_W)
    cs = np.clip(cq - NA_COLS // 2, 0, GRID_W - NA_COLS)
    col_ok = (cq[None, :] >= cs[:, None]) & (cq[None, :] < cs[:, None] + NA_COLS)
    dc = np.clip(cq[None, :] - cq[:, None] + (NA_COLS - 1), 0, 2 * NA_COLS - 2)
    onehot = (dc[None, :, :] == np.arange(2 * NA_COLS - 1)[:, None, None]).astype(np.float32)
    bias = jnp.einsum('lhrij,jqk->lhrqik', by_row.astype(F32), onehot, precision=lax.Precision.HIGHEST)
    bias = jnp.where(col_ok[None, None, None, :, None, :], bias, NEG_INF)
    return bias.reshape(depth, heads, rows * GRID_W, kr * GRID_W)


def _na_lat_kernel(q_ref, k_ref, v_ref, kc_ref, vc_ref, bias_ref, o_ref, *, groups, kr, scale):
    q = q_ref[...].astype(BF16)
    k = k_ref[...].astype(BF16)
    v = v_ref[...].astype(BF16)
    kc = kc_ref[...].astype(BF16)
    vc = vc_ref[...].astype(BF16)
    for r0, nr, kr0 in groups:
        q0, q1 = r0 * GRID_W, (r0 + nr) * GRID_W
        k0, k1 = kr0 * GRID_W, (kr0 + kr) * GRID_W
        qg = q[q0:q1]
        s_lat = _qk(qg, k[k0:k1]) * scale + bias_ref[q0:q1, :]
        s_ctx = _qk(qg, kc) * scale
        o_ref[q0:q1, :] = _softmax_pv([s_lat, s_ctx], [v[k0:k1], vc])


def _na_lat(p, cache_k, cache_v, bias, l, *, n_ctx, batch, t, heads, past, scale):
    rows = t // GRID_W
    kr, groups = _na_groups(rows)
    rb = n_ctx // t
    depth = cache_k.shape[1]
    ck = cache_k.reshape(batch, depth, past, heads * HEAD_DIM)
    cv = cache_v.reshape(batch, depth, past, heads * HEAD_DIM)
    kern = functools.partial(_na_lat_kernel, groups=groups, kr=kr, scale=scale)
    return pl.pallas_call(
        kern,
        grid=(heads, batch),
        in_specs=[
            pl.BlockSpec((t, HEAD_DIM), lambda h, b: (rb + b, h)),
            pl.BlockSpec((t, HEAD_DIM), lambda h, b: (rb + b, heads + h)),
            pl.BlockSpec((t, HEAD_DIM), lambda h, b: (rb + b, 2 * heads + h)),
            pl.BlockSpec((None, None, past, HEAD_DIM), lambda h, b: (b, l, 0, h)),
            pl.BlockSpec((None, None, past, HEAD_DIM), lambda h, b: (b, l, 0, h)),
            pl.BlockSpec((None, None, t, kr * GRID_W), lambda h, b: (l, h, 0, 0)),
        ],
        out_specs=pl.BlockSpec((t, HEAD_DIM), lambda h, b: (b, h)),
        out_shape=jax.ShapeDtypeStruct((batch * t, heads * HEAD_DIM), F32),
        compiler_params=_cparams(("arbitrary", "arbitrary")),
        name="na_latent",
    )(p, p, p, ck, cv, bias)


def _rope_tables(t):
    pos = np.arange(t)
    nf = HEAD_DIM // 4
    inv = jnp.asarray(ROPE_BASE, F32) ** (-jnp.arange(nf, dtype=F32) / nf)
    ang_r = jnp.asarray(pos // GRID_W, F32)[:, None] * inv
    ang_c = jnp.asarray(pos % GRID_W, F32)[:, None] * inv
    cos = jnp.concatenate([jnp.cos(ang_r)] * 2 + [jnp.cos(ang_c)] * 2, axis=-1)
    sin = jnp.concatenate([-jnp.sin(ang_r), jnp.sin(ang_r), -jnp.sin(ang_c), jnp.sin(ang_c)], axis=-1)
    return cos, sin


def _rope(x, cos, sin):
    nf = HEAD_DIM // 4
    lane = lax.broadcasted_iota(jnp.int32, x.shape, 1)
    swapped = jnp.where(lane % (2 * nf) < nf,
                        pltpu.roll(x, HEAD_DIM - nf, 1),
                        pltpu.roll(x, nf, 1))
    return x * cos + swapped * sin


def _swa_lat_kernel(sink_ref, q_ref, k_ref, v_ref, kc_ref, vc_ref, cos_ref, sin_ref, o_ref,
                    krot_ref, vbf_ref, *, group, t, scale):
    kv = pl.program_id(1)
    krot_ref[...] = _rope(k_ref[...], cos_ref[...], sin_ref[...]).astype(BF16)
    vbf_ref[...] = v_ref[...].astype(BF16)
    kc = kc_ref[...].astype(BF16)
    vc = vc_ref[...].astype(BF16)
    span = 3 * SWA_BLOCK

    def block(n, carry):
        q0 = pl.multiple_of(n * SWA_BLOCK, SWA_BLOCK)
        w0 = pl.multiple_of(jnp.clip((n - 1) * SWA_BLOCK, 0, t - span), SWA_BLOCK)
        kw = krot_ref[pl.ds(w0, span), :]
        vw = vbf_ref[pl.ds(w0, span), :]
        qpos = q0 + lax.broadcasted_iota(jnp.int32, (SWA_BLOCK, span), 0)
        kpos = w0 + lax.broadcasted_iota(jnp.int32, (SWA_BLOCK, span), 1)
        valid = jnp.abs(kpos - qpos) <= SWA_WINDOW
        cos = cos_ref[pl.ds(q0, SWA_BLOCK), :]
        sin = sin_ref[pl.ds(q0, SWA_BLOCK), :]
        for g in range(group):
            cols = slice(g * HEAD_DIM, (g + 1) * HEAD_DIM)
            qg = _rope(q_ref[pl.ds(q0, SWA_BLOCK), cols], cos, sin).astype(BF16)
            s_lat = jnp.where(valid, _qk(qg, kw) * scale, NEG_INF)
            s_ctx = _qk(qg, kc) * scale
            o_ref[pl.ds(q0, SWA_BLOCK), cols] = _softmax_pv([s_lat, s_ctx], [vw, vc],
                                                           sink_ref[kv * group + g])
        return carry

    lax.fori_loop(0, t // SWA_BLOCK, block, 0)


def _swa_lat(p, cache_k, cache_v, sink_l, cos, sin, l, *, n_ctx, batch, t, n_kv, group, past,
             q_col, k_col, v_col, scale):
    assert t >= 3 * SWA_BLOCK and t % SWA_BLOCK == 0
    gw = group * HEAD_DIM
    rb = n_ctx // t
    depth = cache_k.shape[1]
    ck = cache_k.reshape(batch, depth, past, n_kv * HEAD_DIM)
    cv = cache_v.reshape(batch, depth, past, n_kv * HEAD_DIM)
    kern = functools.partial(_swa_lat_kernel, group=group, t=t, scale=scale)
    return pl.pallas_call(
        kern,
        grid=(batch, n_kv),
        in_specs=[
            pl.BlockSpec(memory_space=pltpu.SMEM),
            pl.BlockSpec((t, gw), lambda b, h: (rb + b, q_col // group + h)),
            pl.BlockSpec((t, HEAD_DIM), lambda b, h: (rb + b, k_col + h)),
            pl.BlockSpec((t, HEAD_DIM), lambda b, h: (rb + b, v_col + h)),
            pl.BlockSpec((None, None, past, HEAD_DIM), lambda b, h: (b, l, 0, h)),
            pl.BlockSpec((None, None, past, HEAD_DIM), lambda b, h: (b, l, 0, h)),
            pl.BlockSpec((t, HEAD_DIM), lambda b, h: (0, 0)),
            pl.BlockSpec((t, HEAD_DIM), lambda b, h: (0, 0)),
        ],
        out_specs=pl.BlockSpec((t, gw), lambda b, h: (b, h)),
        out_shape=jax.ShapeDtypeStruct((batch * t, n_kv * gw), F32),
        scratch_shapes=[pltpu.VMEM((t, HEAD_DIM), BF16), pltpu.VMEM((t, HEAD_DIM), BF16)],
        compiler_params=_cparams(("arbitrary", "arbitrary")),
        name="swa_latent",
    )(sink_l, p, p, p, ck, cv, cos, sin)


def _topk_rows(v, k):
    n = v.shape[0]
    iota = lax.broadcasted_iota(jnp.int32, v.shape, 0)
    vals, idxs = [], []
    for _ in range(k):
        m = jnp.max(v, axis=0, keepdims=True)
        first = jnp.min(jnp.where(v == m, iota, n), axis=0, keepdims=True)
        v = jnp.where(iota == first, -jnp.inf, v)
        vals.append(m)
        idxs.append(first)
    return jnp.concatenate(vals, axis=0), jnp.concatenate(idxs, axis=0)


def _peer_topk_kernel(q_ref, sk_ref, idx_ref, g_ref, *, n_keys, key_dim):
    k = PEER_TOPK
    sv, si = [], []
    for part in range(2):
        q = q_ref[:, part * key_dim:(part + 1) * key_dim].astype(BF16)
        s = _qk(sk_ref[part].astype(BF16), q)
        v, i = _topk_rows(s, k)
        sv.append(v)
        si.append(i)
    cand = jnp.concatenate([sv[0][a:a + 1] + sv[1] for a in range(k)], axis=0)
    cv, cp = _topk_rows(cand, k)
    ca, cb = cp // k, cp % k
    e1 = jnp.zeros_like(cp)
    e2 = jnp.zeros_like(cp)
    for a in range(k):
        e1 = jnp.where(ca == a, si[0][a:a + 1], e1)
        e2 = jnp.where(cb == a, si[1][a:a + 1], e2)
    idx_ref[...] = e1 * n_keys + e2
    e = jnp.exp(cv - jnp.max(cv, axis=0, keepdims=True))
    g_ref[...] = e / jnp.sum(e, axis=0, keepdims=True)


def _peer_topk(q, subkeys, l):
    n = q.shape[0]
    depth, heads, _, n_keys, key_dim = subkeys.shape
    tq = 256
    k = PEER_TOPK
    kern = functools.partial(_peer_topk_kernel, n_keys=n_keys, key_dim=key_dim)
    idx, g = pl.pallas_call(
        kern,
        grid=(n // tq, heads),
        in_specs=[
            pl.BlockSpec((tq, 2 * key_dim), lambda i, h: (i, h)),
            pl.BlockSpec((None, None, 2, n_keys, key_dim), lambda i, h: (l, h, 0, 0, 0)),
        ],
        out_specs=[pl.BlockSpec((None, k, tq), lambda i, h: (h, 0, i))] * 2,
        out_shape=[jax.ShapeDtypeStruct((heads, k, n), jnp.int32),
                   jax.ShapeDtypeStruct((heads, k, n), F32)],
        compiler_params=_cparams(("arbitrary", "arbitrary")),
        name="peer_topk",
    )(q, subkeys)
    return idx.reshape(heads * k, n), g.reshape(heads * k, n)


PEER_TOKENS = 128
PEER_SLOTS = 8
PEER_DTYPE = BF16


def _expert_table(u, v):
    depth, e, d = u.shape
    uv = jnp.concatenate([u, v], axis=-1).astype(PEER_DTYPE)
    return uv.reshape(depth, e, 2 * d // LANES, LANES)


def _peer_kernel(idx_ref, g_ref, x_ref, res_ref, mod_ref, uv_hbm, o_ref, *scratch,
                 l, kk, n_ctx, dec_seq, gate_idx):
    tb, ns = PEER_TOKENS, PEER_SLOTS
    bufs = scratch[:ns]
    hbuf, abuf, xs, ys, sem = scratch[ns:]
    nc = xs.shape[0]
    nv = nc // SUBLANES
    n_groups = kk // SUBLANES
    step = pl.program_id(0)
    gate = mod_ref[_mod_index(step * tb, n_ctx, dec_seq), pl.ds(gate_idx, 1), :]
    sub = lax.broadcasted_iota(jnp.int32, (SUBLANES, LANES), 0)
    lane = lax.broadcasted_iota(jnp.int32, (SUBLANES, LANES), 1)

    def issue_group(t, slot, kg, js=range(SUBLANES)):
        base = t * kk + kg * SUBLANES
        for j in js:
            e = idx_ref[base + j]
            pltpu.make_async_copy(uv_hbm.at[l, e], bufs[slot].at[kg * SUBLANES + j],
                                  sem.at[slot]).start(priority=j % 2)

    def wait_slot(slot):
        pltpu.make_async_copy(uv_hbm.at[l, pl.ds(0, kk)], bufs[slot], sem.at[slot]).wait()

    def fold_sublanes(parts):
        dist = SUBLANES // 2
        while len(parts) > 1:
            low = (sub & dist) == 0
            half = len(parts) // 2
            nxt = []
            for i in range(half):
                keep = jnp.where(low, parts[i], parts[i + half])
                move = jnp.where(low, parts[i + half], parts[i])
                up = pltpu.roll(move, SUBLANES - dist, 0)
                swapped = up if 2 * dist == SUBLANES else jnp.where(low, up, pltpu.roll(move, dist, 0))
                nxt.append(keep + swapped)
            parts = nxt
            dist //= 2
        return parts[0]

    def pre_activations(t, slot, fetch):
        buf = bufs[slot]
        xrow = x_ref[pl.ds(t, 1), :]
        for c in range(nc):
            xs[c:c + 1, :] = xrow[:, c * LANES:(c + 1) * LANES]
        xv = [xs[i * SUBLANES:(i + 1) * SUBLANES, :] for i in range(nv)]
        hmat = jnp.zeros((SUBLANES, LANES), F32)
        for kg in range(n_groups):
            fetch(kg)
            parts = []
            for j in range(SUBLANES):
                uk = buf[kg * SUBLANES + j, 0:nc, :].astype(F32)
                s = uk[0:SUBLANES] * xv[0]
                for i in range(1, nv):
                    s = s + uk[i * SUBLANES:(i + 1) * SUBLANES] * xv[i]
                parts.append(s)
            hk = jnp.sum(fold_sublanes(parts), axis=-1, keepdims=True)
            hmat = jnp.where(lane == kg, hk, hmat)
        hbuf[...] = hmat

    def activations(t):
        hmat = hbuf[...]
        amat = g_ref[t] * (0.5 * hmat * (1.0 + lax.erf(hmat * np.float32(0.5 ** 0.5))))
        for kg in range(n_groups):
            abuf[kg * SUBLANES:(kg + 1) * SUBLANES, :] = jnp.broadcast_to(amat[:, kg:kg + 1], (SUBLANES, LANES))

    def combine(t, slot, fetch):
        buf = bufs[slot]
        acc = [jnp.zeros((SUBLANES, LANES), F32)] * (2 * nv)
        for k in range(kk):
            if k % SUBLANES == 0:
                fetch(k // SUBLANES)
            ak = jnp.broadcast_to(abuf[k:k + 1, :], (SUBLANES, LANES))
            vk = buf[k, nc:2 * nc, :].astype(F32)
            for i in range(nv):
                n = (k % 2) * nv + i
                acc[n] = acc[n] + vk[i * SUBLANES:(i + 1) * SUBLANES] * ak
        for i in range(nv):
            ys[i * SUBLANES:(i + 1) * SUBLANES, :] = acc[i] + acc[nv + i]
        y = jnp.concatenate([ys[c:c + 1, :] for c in range(nc)], axis=1)
        o_ref[pl.ds(t, 1), :] = res_ref[pl.ds(t, 1), :] + gate * y

    def stage(s, j, do_issue, do_act):
        half = SUBLANES // 2
        nothing = lambda kg: None
        fetch_lo = fetch_hi = nothing
        if do_issue:
            fetch_lo = lambda kg: issue_group(s + ns - 1, (j + ns - 1) % ns, kg, range(half))
            fetch_hi = lambda kg: issue_group(s + ns - 1, (j + ns - 1) % ns, kg, range(half, SUBLANES))
        if do_act:
            wait_slot((j + 1) % ns)
        activations(s)
        if do_act:
            pre_activations(s + 1, (j + 1) % ns, fetch_lo)
        else:
            for kg in range(n_groups):
                fetch_lo(kg)
        combine(s, j, fetch_hi)

    for t in range(ns - 1):
        lax.fori_loop(0, n_groups, lambda kg, c, t=t: (issue_group(t, t, kg), c)[1], 0)
    wait_slot(0)
    pre_activations(0, 0, lambda kg: None)
    n_rounds = tb // ns

    def full_round(r, c):
        for j in range(ns):
            stage(r * ns + j, j, True, True)
        return c
    lax.fori_loop(0, n_rounds - 1, full_round, 0)
    for j in range(ns):
        stage((n_rounds - 1) * ns + j, j, j == 0, j < ns - 1)


def _peer(idx, g, x, res, mod, uv, l, gate_idx, n_ctx, dec_seq):
    n, d = x.shape
    kk = idx.shape[1]
    r = mod.shape[1]
    nc = d // LANES
    tb, ns = PEER_TOKENS, PEER_SLOTS
    n_groups = kk // SUBLANES
    assert kk % SUBLANES == 0 and n_groups <= LANES and n % tb == 0 and tb % ns == 0 and ns % 2 == 0
    assert nc % (2 * SUBLANES) == 0
    g_tiles = jnp.pad(g.reshape(n_groups, SUBLANES, n).transpose(2, 1, 0),
                      ((0, 0), (0, 0), (0, LANES - n_groups)))
    kern = functools.partial(_peer_kernel, l=l, kk=kk, n_ctx=n_ctx, dec_seq=dec_seq, gate_idx=gate_idx)
    return pl.pallas_call(
        kern,
        grid=(n // tb,),
        in_specs=[
            pl.BlockSpec((tb * kk,), lambda i: (i,), memory_space=pltpu.SMEM),
            pl.BlockSpec((tb, SUBLANES, LANES), lambda i: (i, 0, 0)),
            pl.BlockSpec((tb, d), lambda i: (i, 0)),
            pl.BlockSpec((tb, d), lambda i: (i, 0)),
            pl.BlockSpec((None, r, 6, d), lambda i: (l, 0, 0, 0)),
            pl.BlockSpec(memory_space=pl.ANY),
        ],
        out_specs=pl.BlockSpec((tb, d), lambda i: (i, 0)),
        out_shape=jax.ShapeDtypeStruct((n, d), F32),
        scratch_shapes=[pltpu.VMEM((kk, 2 * nc, LANES), uv.dtype) for _ in range(ns)] + [
            pltpu.VMEM((SUBLANES, LANES), F32),
            pltpu.VMEM((kk, LANES), F32),
            pltpu.VMEM((nc, LANES), F32),
            pltpu.VMEM((nc, LANES), F32),
            pltpu.SemaphoreType.DMA((ns,)),
        ],
        compiler_params=_cparams(("arbitrary",)),
        name="peer_experts",
    )(idx.reshape(n * kk), g_tiles, x, res, mod, uv)


def kernel(x_prompt, x_sample, cache_na_k, cache_na_v, cache_swa_k, cache_swa_v, c, c_ctx,
           w_mod, b_mod, norm1_g, w_in, rpb, sink, out_norm_a, out_norm_b, w_out, norm2_g,
           peer_wq, peer_subkeys, peer_u, peer_v, final_g):
    batch, seq, d = x_prompt.shape
    dec_batch, dec_seq, _ = x_sample.shape
    depth = w_mod.shape[0]
    past = cache_na_k.shape[2]
    na_heads = cache_na_k.shape[3]
    swa_kv = cache_swa_k.shape[3]
    swa_heads = sink.shape[1]
    group = swa_heads // swa_kv
    n_ctx = batch * seq
    scale = HEAD_DIM ** -0.5
    qa_col, ka_col, va_col = 0, na_heads, 2 * na_heads
    qb_col = 3 * na_heads
    kb_col = qb_col + swa_heads
    vb_col = kb_col + swa_kv

    x = jnp.concatenate([x_prompt.reshape(n_ctx, d), x_sample.reshape(dec_batch * dec_seq, d)], axis=0)

    n_mod = 1 + dec_batch
    r_mod = -(-n_mod // SUBLANES) * SUBLANES
    cond = jnp.concatenate([c_ctx[None, :], c, jnp.zeros((r_mod - n_mod, d), F32)], axis=0)
    mod = _adaln(cond, w_mod, b_mod).reshape(depth, r_mod, 6, d)

    cos, sin = _rope_tables(dec_seq)
    na_bias = _na_bias(rpb, dec_seq // GRID_W)
    uv_tab = _expert_table(peer_u, peer_v)
    na_k, na_v, swa_k, swa_v = [], [], [], []
    for l in range(depth):
        (h,) = _norm_mod(x, norm1_g, mod, l, 0, 1, n_ctx, dec_seq, [BF16])
        p = _matmul(h, w_in, l, n_ctx, dec_seq)
        oa_ctx = _ctx_attn(p, None, batch=batch, seq=seq, n_kv=na_heads, group=1,
                           q_col=qa_col, k_col=ka_col, v_col=va_col, scale=scale)
        ob_ctx = _ctx_attn(p, sink[l], batch=batch, seq=seq, n_kv=swa_kv, group=group,
                           q_col=qb_col, k_col=kb_col, v_col=vb_col, scale=scale)
        oa_lat = _na_lat(p, cache_na_k, cache_na_v, na_bias, l,
                         n_ctx=n_ctx, batch=dec_batch, t=dec_seq, heads=na_heads, past=past, scale=scale)
        ob_lat = _swa_lat(p, cache_swa_k, cache_swa_v, sink[l], cos, sin, l,
                          n_ctx=n_ctx, batch=dec_batch, t=dec_seq, n_kv=swa_kv, group=group, past=past,
                          q_col=qb_col, k_col=kb_col, v_col=vb_col, scale=scale)
        hn = _group_norm(oa_ctx, ob_ctx, oa_lat, ob_lat, out_norm_a, out_norm_b, l)
        x = _matmul_residual(hn, w_out, x, mod, l, 2, n_ctx, dec_seq)

        h2_bf, h2 = _norm_mod(x, norm2_g, mod, l, 3, 4, n_ctx, dec_seq, [BF16, F32])
        q = _matmul(h2_bf, peer_wq, l, n_ctx, dec_seq)
        idx_t, g_t = _peer_topk(q, peer_subkeys, l)
        x = _peer(idx_t.T, g_t, h2, x, mod, uv_tab, l, 5, n_ctx, dec_seq)

        pc = p[:n_ctx]
        na_k.append(pc[:, ka_col * HEAD_DIM:va_col * HEAD_DIM].reshape(batch, seq, na_heads, HEAD_DIM))
        na_v.append(pc[:, va_col * HEAD_DIM:qb_col * HEAD_DIM].reshape(batch, seq, na_heads, HEAD_DIM))
        swa_k.append(pc[:, kb_col * HEAD_DIM:vb_col * HEAD_DIM].reshape(batch, seq, swa_kv, HEAD_DIM))
        swa_v.append(pc[:, vb_col * HEAD_DIM:].reshape(batch, seq, swa_kv, HEAD_DIM))

    y = _final_norm(x, final_g)
    return (y[:n_ctx].reshape(batch, seq, d), y[n_ctx:].reshape(dec_batch, dec_seq, d),
            jnp.stack(na_k, axis=1), jnp.stack(na_v, axis=1),
            jnp.stack(swa_k, axis=1), jnp.stack(swa_v, axis=1))
```

```python
import functools

import jax
import jax.numpy as jnp
import numpy as np
from jax import lax
from jax.experimental import pallas as pl
from jax.experimental.pallas import tpu as pltpu

GRID_W = 64
NA_ROWS = 8
NA_COLS = 16
SWA_WINDOW = 128
SWA_BLOCK = 128
ROPE_BASE = 10000.0
PEER_TOPK = 16
EPS = 1e-6
NEG_INF = -1e30

HEAD_DIM = 128
LANES = 128
SUBLANES = 8
VMEM_LIMIT = 56 * 1024 * 1024

F32 = jnp.float32
BF16 = jnp.bfloat16


def _cparams(sem):
    return pltpu.CompilerParams(dimension_semantics=sem, vmem_limit_bytes=VMEM_LIMIT)


def _mod_index(row0, n_ctx, dec_seq):
    return jnp.where(row0 < n_ctx, 0, 1 + (row0 - n_ctx) // dec_seq)


def _adaln_kernel(c_ref, w_ref, b_ref, o_ref):
    c = c_ref[...]
    s = (c * jax.nn.sigmoid(c)).astype(BF16)
    o_ref[...] = jnp.dot(s, w_ref[...].astype(BF16), preferred_element_type=F32) + b_ref[...]


def _adaln(cond, w_mod, b_mod):
    depth, d, n = w_mod.shape
    r = cond.shape[0]
    tn = next(t for t in (1024, 512, 256, LANES) if n % t == 0)
    return pl.pallas_call(
        _adaln_kernel,
        grid=(depth, n // tn),
        in_specs=[
            pl.BlockSpec((r, d), lambda l, j: (0, 0)),
            pl.BlockSpec((None, d, tn), lambda l, j: (l, 0, j)),
            pl.BlockSpec((None, 1, tn), lambda l, j: (l, 0, j)),
        ],
        out_specs=pl.BlockSpec((None, r, tn), lambda l, j: (l, 0, j)),
        out_shape=jax.ShapeDtypeStruct((depth, r, n), F32),
        compiler_params=_cparams(("arbitrary", "arbitrary")),
        name="adaln",
    )(cond, w_mod, b_mod.reshape(depth, 1, n))


def _norm_mod_kernel(x_ref, g_ref, mod_ref, *o_refs, tr, n_ctx, dec_seq, shift_idx, scale_idx):
    x = x_ref[...]
    y = x * lax.rsqrt(jnp.mean(x * x, axis=-1, keepdims=True) + EPS) * g_ref[...]
    m = _mod_index(pl.program_id(0) * tr, n_ctx, dec_seq)
    shift = mod_ref[m, pl.ds(shift_idx, 1), :]
    scale = mod_ref[m, pl.ds(scale_idx, 1), :]
    h = y * (1.0 + scale) + shift
    for o_ref in o_refs:
        o_ref[...] = h.astype(o_ref.dtype)


def _norm_mod(x, g, mod, l, shift_idx, scale_idx, n_ctx, dec_seq, out_dtypes):
    n, d = x.shape
    tr = 256
    depth, r = mod.shape[:2]
    kern = functools.partial(_norm_mod_kernel, tr=tr, n_ctx=n_ctx, dec_seq=dec_seq,
                             shift_idx=shift_idx, scale_idx=scale_idx)
    outs = pl.pallas_call(
        kern,
        grid=(n // tr,),
        in_specs=[
            pl.BlockSpec((tr, d), lambda i: (i, 0)),
            pl.BlockSpec((None, 1, d), lambda i: (l, 0, 0)),
            pl.BlockSpec((None, r, 6, d), lambda i: (l, 0, 0, 0)),
        ],
        out_specs=[pl.BlockSpec((tr, d), lambda i: (i, 0)) for _ in out_dtypes],
        out_shape=[jax.ShapeDtypeStruct((n, d), dt) for dt in out_dtypes],
        compiler_params=_cparams(("arbitrary",)),
        name="norm_mod",
    )(x, g.reshape(depth, 1, d), mod)
    return outs


def _final_norm_kernel(x_ref, g_ref, o_ref):
    x = x_ref[...]
    o_ref[...] = x * lax.rsqrt(jnp.mean(x * x, axis=-1, keepdims=True) + EPS) * g_ref[...]


def _final_norm(x, g):
    n, d = x.shape
    tr = 256
    return pl.pallas_call(
        _final_norm_kernel,
        grid=(n // tr,),
        in_specs=[pl.BlockSpec((tr, d), lambda i: (i, 0)), pl.BlockSpec((1, d), lambda i: (0, 0))],
        out_specs=pl.BlockSpec((tr, d), lambda i: (i, 0)),
        out_shape=jax.ShapeDtypeStruct((n, d), F32),
        compiler_params=_cparams(("arbitrary",)),
        name="final_norm",
    )(x, g.reshape(1, d))


def _group_norm_kernel(oac_ref, obc_ref, oal_ref, obl_ref, ga_ref, gb_ref, h_ref, *, mix_a, ctx_tiles):
    def nrm(x, g):
        return (x * lax.rsqrt(jnp.mean(x * x, axis=-1, keepdims=True) + EPS) * g).astype(h_ref.dtype)

    def emit(oa_ref, ob_ref):
        h_ref[:, :mix_a] = nrm(oa_ref[...], ga_ref[...])
        h_ref[:, mix_a:] = nrm(ob_ref[...], gb_ref[...])

    is_ctx = pl.program_id(0) < ctx_tiles
    pl.when(is_ctx)(lambda: emit(oac_ref, obc_ref))
    pl.when(jnp.logical_not(is_ctx))(lambda: emit(oal_ref, obl_ref))


def _group_norm(oa_ctx, ob_ctx, oa_lat, ob_lat, ga, gb, l):
    n_ctx, mix_a = oa_ctx.shape
    n_lat, mix_b = ob_lat.shape
    depth = ga.shape[0]
    tr = 256
    ctx_tiles = n_ctx // tr
    lat_tiles = n_lat // tr
    ctx_map = lambda i: (jnp.minimum(i, ctx_tiles - 1), 0)
    lat_map = lambda i: (jnp.maximum(i - ctx_tiles, 0), 0)
    return pl.pallas_call(
        functools.partial(_group_norm_kernel, mix_a=mix_a, ctx_tiles=ctx_tiles),
        grid=(ctx_tiles + lat_tiles,),
        in_specs=[
            pl.BlockSpec((tr, mix_a), ctx_map),
            pl.BlockSpec((tr, mix_b), ctx_map),
            pl.BlockSpec((tr, mix_a), lat_map),
            pl.BlockSpec((tr, mix_b), lat_map),
            pl.BlockSpec((None, 1, mix_a), lambda i: (l, 0, 0)),
            pl.BlockSpec((None, 1, mix_b), lambda i: (l, 0, 0)),
        ],
        out_specs=pl.BlockSpec((tr, mix_a + mix_b), lambda i: (i, 0)),
        out_shape=jax.ShapeDtypeStruct((n_ctx + n_lat, mix_a + mix_b), BF16),
        compiler_params=_cparams(("arbitrary",)),
        name="group_norm",
    )(oa_ctx, ob_ctx, oa_lat, ob_lat, ga.reshape(depth, 1, mix_a), gb.reshape(depth, 1, mix_b))


def _mm_kernel(a_ref, w_ref, o_ref):
    o_ref[...] = jnp.dot(a_ref[...], w_ref[...].astype(BF16), preferred_element_type=F32)


def _mm_res_kernel(a_ref, w_ref, x_ref, mod_ref, o_ref, *, tm, n_ctx, dec_seq, gate_idx):
    acc = jnp.dot(a_ref[...], w_ref[...].astype(BF16), preferred_element_type=F32)
    m = _mod_index(pl.program_id(0) * tm, n_ctx, dec_seq)
    gate = mod_ref[m, pl.ds(gate_idx, 1), :]
    o_ref[...] = x_ref[...] + gate * acc


def _row_tile(n_ctx, dec_seq):
    tm = min(1024, dec_seq)
    assert n_ctx % tm == 0 and dec_seq % tm == 0
    return tm


def _col_tile(m):
    return next(t for t in (512, 256, LANES) if m % t == 0)


def _matmul(a, w, l, n_ctx, dec_seq):
    n, k = a.shape
    m = w.shape[2]
    tm = _row_tile(n_ctx, dec_seq)
    tn = _col_tile(m)
    return pl.pallas_call(
        _mm_kernel,
        grid=(n // tm, m // tn),
        in_specs=[
            pl.BlockSpec((tm, k), lambda i, j: (i, 0)),
            pl.BlockSpec((None, k, tn), lambda i, j: (l, 0, j)),
        ],
        out_specs=pl.BlockSpec((tm, tn), lambda i, j: (i, j)),
        out_shape=jax.ShapeDtypeStruct((n, m), F32),
        compiler_params=_cparams(("arbitrary", "arbitrary")),
        name="matmul",
    )(a, w)


def _matmul_residual(a, w, x, mod, l, gate_idx, n_ctx, dec_seq):
    n, k = a.shape
    m = w.shape[2]
    r = mod.shape[1]
    tm = _row_tile(n_ctx, dec_seq)
    tn = _col_tile(m)
    kern = functools.partial(_mm_res_kernel, tm=tm, n_ctx=n_ctx, dec_seq=dec_seq, gate_idx=gate_idx)
    return pl.pallas_call(
        kern,
        grid=(n // tm, m // tn),
        in_specs=[
            pl.BlockSpec((tm, k), lambda i, j: (i, 0)),
            pl.BlockSpec((None, k, tn), lambda i, j: (l, 0, j)),
            pl.BlockSpec((tm, tn), lambda i, j: (i, j)),
            pl.BlockSpec((None, r, 6, tn), lambda i, j: (l, 0, 0, j)),
        ],
        out_specs=pl.BlockSpec((tm, tn), lambda i, j: (i, j)),
        out_shape=jax.ShapeDtypeStruct((n, m), F32),
        compiler_params=_cparams(("arbitrary", "arbitrary")),
        name="matmul_residual",
    )(a, w, x, mod)


_NT = (((1,), (1,)), ((), ()))


def _qk(q, k):
    return lax.dot_general(q, k, _NT, preferred_element_type=F32)


def _pv(p, v):
    return jnp.dot(p.astype(BF16), v, preferred_element_type=F32)


def _softmax_pv(score_blocks, value_blocks, sink=None):
    m = functools.reduce(jnp.maximum, [jnp.max(s, axis=-1, keepdims=True) for s in score_blocks])
    if sink is not None:
        m = jnp.maximum(m, sink)
    ps = [jnp.exp(s - m) for s in score_blocks]
    denom = functools.reduce(jnp.add, [jnp.sum(p, axis=-1, keepdims=True) for p in ps])
    if sink is not None:
        denom = denom + jnp.exp(sink - m)
    o = functools.reduce(jnp.add, [_pv(p, v) for p, v in zip(ps, value_blocks)])
    return o / denom


def _ctx_attn_kernel(*refs, group, scale, has_sink):
    if has_sink:
        sink_ref, q_ref, k_ref, v_ref, o_ref = refs
    else:
        q_ref, k_ref, v_ref, o_ref = refs
    k = k_ref[...].astype(BF16)
    v = v_ref[...].astype(BF16)
    kv = pl.program_id(1)
    for g in range(group):
        q = q_ref[:, g * HEAD_DIM:(g + 1) * HEAD_DIM].astype(BF16)
        s = _qk(q, k) * scale
        sink = sink_ref[kv * group + g] if has_sink else None
        o_ref[:, g * HEAD_DIM:(g + 1) * HEAD_DIM] = _softmax_pv([s], [v], sink)


def _ctx_attn(p, sink_l, *, batch, seq, n_kv, group, q_col, k_col, v_col, scale):
    gw = group * HEAD_DIM
    has_sink = sink_l is not None
    kern = functools.partial(_ctx_attn_kernel, group=group, scale=scale, has_sink=has_sink)
    in_specs = [
        pl.BlockSpec((seq, gw), lambda b, h: (b, q_col // group + h)),
        pl.BlockSpec((seq, HEAD_DIM), lambda b, h: (b, k_col + h)),
        pl.BlockSpec((seq, HEAD_DIM), lambda b, h: (b, v_col + h)),
    ]
    args = [p, p, p]
    if has_sink:
        in_specs = [pl.BlockSpec(memory_space=pltpu.SMEM)] + in_specs
        args = [sink_l] + args
    return pl.pallas_call(
        kern,
        grid=(batch, n_kv),
        in_specs=in_specs,
        out_specs=pl.BlockSpec((seq, gw), lambda b, h: (b, h)),
        out_shape=jax.ShapeDtypeStruct((batch * seq, n_kv * gw), F32),
        compiler_params=_cparams(("arbitrary", "arbitrary")),
        name="ctx_attn_sink" if has_sink else "ctx_attn",
    )(*args)


def _na_groups(rows):
    kr = min(NA_ROWS, rows)
    starts = np.clip(np.arange(rows) - kr // 2, 0, rows - kr)
    groups, r = [], 0
    while r < rows:
        e = r
        while e + 1 < rows and starts[e + 1] == starts[r]:
            e += 1
        groups.append((r, e - r + 1, int(starts[r])))
        r = e + 1
    return kr, groups


def _na_bias(rpb, rows):
    depth, heads = rpb.shape[:2]
    kr = min(NA_ROWS, rows)
    r = np.arange(rows)
    d0 = np.clip(r - kr // 2, 0, rows - kr) - r + (NA_ROWS - 1)
    by_row = jnp.stack([rpb[:, :, int(s):int(s) + kr, :] for s in d0], axis=2)
    cq = np.arange(GRID_W)
    cs = np.clip(cq - NA_COLS // 2, 0, GRID_W - NA_COLS)
    col_ok = (cq[None, :] >= cs[:, None]) & (cq[None, :] < cs[:, None] + NA_COLS)
    dc = np.clip(cq[None, :] - cq[:, None] + (NA_COLS - 1), 0, 2 * NA_COLS - 2)
    onehot = (dc[None, :, :] == np.arange(2 * NA_COLS - 1)[:, None, None]).astype(np.float32)
    bias = jnp.einsum('lhrij,jqk->lhrqik', by_row.astype(F32), onehot, precision=lax.Precision.HIGHEST)
    bias = jnp.where(col_ok[None, None, None, :, None, :], bias, NEG_INF)
    return bias.reshape(depth, heads, rows * GRID_W, kr * GRID_W)


def _na_lat_kernel(q_ref, k_ref, v_ref, kc_ref, vc_ref, bias_ref, o_ref, *, groups, kr, scale):
    q = q_ref[...].astype(BF16)
    k = k_ref[...].astype(BF16)
    v = v_ref[...].astype(BF16)
    kc = kc_ref[...].astype(BF16)
    vc = vc_ref[...].astype(BF16)
    for r0, nr, kr0 in groups:
        q0, q1 = r0 * GRID_W, (r0 + nr) * GRID_W
        k0, k1 = kr0 * GRID_W, (kr0 + kr) * GRID_W
        qg = q[q0:q1]
        s_lat = _qk(qg, k[k0:k1]) * scale + bias_ref[q0:q1, :]
        s_ctx = _qk(qg, kc) * scale
        o_ref[q0:q1, :] = _softmax_pv([s_lat, s_ctx], [v[k0:k1], vc])


def _na_lat(p, cache_k, cache_v, bias, l, *, n_ctx, batch, t, heads, past, scale):
    rows = t // GRID_W
    kr, groups = _na_groups(rows)
    rb = n_ctx // t
    depth = cache_k.shape[1]
    ck = cache_k.reshape(batch, depth, past, heads * HEAD_DIM)
    cv = cache_v.reshape(batch, depth, past, heads * HEAD_DIM)
    kern = functools.partial(_na_lat_kernel, groups=groups, kr=kr, scale=scale)
    return pl.pallas_call(
        kern,
        grid=(heads, batch),
        in_specs=[
            pl.BlockSpec((t, HEAD_DIM), lambda h, b: (rb + b, h)),
            pl.BlockSpec((t, HEAD_DIM), lambda h, b: (rb + b, heads + h)),
            pl.BlockSpec((t, HEAD_DIM), lambda h, b: (rb + b, 2 * heads + h)),
            pl.BlockSpec((None, None, past, HEAD_DIM), lambda h, b: (b, l, 0, h)),
            pl.BlockSpec((None, None, past, HEAD_DIM), lambda h, b: (b, l, 0, h)),
            pl.BlockSpec((None, None, t, kr * GRID_W), lambda h, b: (l, h, 0, 0)),
        ],
        out_specs=pl.BlockSpec((t, HEAD_DIM), lambda h, b: (b, h)),
        out_shape=jax.ShapeDtypeStruct((batch * t, heads * HEAD_DIM), F32),
        compiler_params=_cparams(("arbitrary", "arbitrary")),
        name="na_latent",
    )(p, p, p, ck, cv, bias)


def _rope_tables(t):
    pos = np.arange(t)
    nf = HEAD_DIM // 4
    inv = jnp.asarray(ROPE_BASE, F32) ** (-jnp.arange(nf, dtype=F32) / nf)
    ang_r = jnp.asarray(pos // GRID_W, F32)[:, None] * inv
    ang_c = jnp.asarray(pos % GRID_W, F32)[:, None] * inv
    cos = jnp.concatenate([jnp.cos(ang_r)] * 2 + [jnp.cos(ang_c)] * 2, axis=-1)
    sin = jnp.concatenate([-jnp.sin(ang_r), jnp.sin(ang_r), -jnp.sin(ang_c), jnp.sin(ang_c)], axis=-1)
    return cos, sin


def _rope(x, cos, sin):
    nf = HEAD_DIM // 4
    lane = lax.broadcasted_iota(jnp.int32, x.shape, 1)
    swapped = jnp.where(lane % (2 * nf) < nf,
                        pltpu.roll(x, HEAD_DIM - nf, 1),
                        pltpu.roll(x, nf, 1))
    return x * cos + swapped * sin


def _swa_lat_kernel(sink_ref, q_ref, k_ref, v_ref, kc_ref, vc_ref, cos_ref, sin_ref, o_ref,
                    krot_ref, vbf_ref, *, group, t, scale):
    kv = pl.program_id(1)
    krot_ref[...] = _rope(k_ref[...], cos_ref[...], sin_ref[...]).astype(BF16)
    vbf_ref[...] = v_ref[...].astype(BF16)
    kc = kc_ref[...].astype(BF16)
    vc = vc_ref[...].astype(BF16)
    span = 3 * SWA_BLOCK

    def block(n, carry):
        q0 = pl.multiple_of(n * SWA_BLOCK, SWA_BLOCK)
        w0 = pl.multiple_of(jnp.clip((n - 1) * SWA_BLOCK, 0, t - span), SWA_BLOCK)
        kw = krot_ref[pl.ds(w0, span), :]
        vw = vbf_ref[pl.ds(w0, span), :]
        qpos = q0 + lax.broadcasted_iota(jnp.int32, (SWA_BLOCK, span), 0)
        kpos = w0 + lax.broadcasted_iota(jnp.int32, (SWA_BLOCK, span), 1)
        valid = jnp.abs(kpos - qpos) <= SWA_WINDOW
        cos = cos_ref[pl.ds(q0, SWA_BLOCK), :]
        sin = sin_ref[pl.ds(q0, SWA_BLOCK), :]
        for g in range(group):
            cols = slice(g * HEAD_DIM, (g + 1) * HEAD_DIM)
            qg = _rope(q_ref[pl.ds(q0, SWA_BLOCK), cols], cos, sin).astype(BF16)
            s_lat = jnp.where(valid, _qk(qg, kw) * scale, NEG_INF)
            s_ctx = _qk(qg, kc) * scale
            o_ref[pl.ds(q0, SWA_BLOCK), cols] = _softmax_pv([s_lat, s_ctx], [vw, vc],
                                                           sink_ref[kv * group + g])
        return carry

    lax.fori_loop(0, t // SWA_BLOCK, block, 0)


def _swa_lat(p, cache_k, cache_v, sink_l, cos, sin, l, *, n_ctx, batch, t, n_kv, group, past,
             q_col, k_col, v_col, scale):
    assert t >= 3 * SWA_BLOCK and t % SWA_BLOCK == 0
    gw = group * HEAD_DIM
    rb = n_ctx // t
    depth = cache_k.shape[1]
    ck = cache_k.reshape(batch, depth, past, n_kv * HEAD_DIM)
    cv = cache_v.reshape(batch, depth, past, n_kv * HEAD_DIM)
    kern = functools.partial(_swa_lat_kernel, group=group, t=t, scale=scale)
    return pl.pallas_call(
        kern,
        grid=(batch, n_kv),
        in_specs=[
            pl.BlockSpec(memory_space=pltpu.SMEM),
            pl.BlockSpec((t, gw), lambda b, h: (rb + b, q_col // group + h)),
            pl.BlockSpec((t, HEAD_DIM), lambda b, h: (rb + b, k_col + h)),
            pl.BlockSpec((t, HEAD_DIM), lambda b, h: (rb + b, v_col + h)),
            pl.BlockSpec((None, None, past, HEAD_DIM), lambda b, h: (b, l, 0, h)),
            pl.BlockSpec((None, None, past, HEAD_DIM), lambda b, h: (b, l, 0, h)),
            pl.BlockSpec((t, HEAD_DIM), lambda b, h: (0, 0)),
            pl.BlockSpec((t, HEAD_DIM), lambda b, h: (0, 0)),
        ],
        out_specs=pl.BlockSpec((t, gw), lambda b, h: (b, h)),
        out_shape=jax.ShapeDtypeStruct((batch * t, n_kv * gw), F32),
        scratch_shapes=[pltpu.VMEM((t, HEAD_DIM), BF16), pltpu.VMEM((t, HEAD_DIM), BF16)],
        compiler_params=_cparams(("arbitrary", "arbitrary")),
        name="swa_latent",
    )(sink_l, p, p, p, ck, cv, cos, sin)


def _topk_rows(v, k):
    n = v.shape[0]
    iota = lax.broadcasted_iota(jnp.int32, v.shape, 0)
    vals, idxs = [], []
    for _ in range(k):
        m = jnp.max(v, axis=0, keepdims=True)
        first = jnp.min(jnp.where(v == m, iota, n), axis=0, keepdims=True)
        v = jnp.where(iota == first, -jnp.inf, v)
        vals.append(m)
        idxs.append(first)
    return jnp.concatenate(vals, axis=0), jnp.concatenate(idxs, axis=0)


def _peer_topk_kernel(q_ref, sk_ref, idx_ref, g_ref, *, n_keys, key_dim):
    k = PEER_TOPK
    sv, si = [], []
    for part in range(2):
        q = q_ref[:, part * key_dim:(part + 1) * key_dim].astype(BF16)
        s = _qk(sk_ref[part].astype(BF16), q)
        v, i = _topk_rows(s, k)
        sv.append(v)
        si.append(i)
    cand = jnp.concatenate([sv[0][a:a + 1] + sv[1] for a in range(k)], axis=0)
    cv, cp = _topk_rows(cand, k)
    ca, cb = cp // k, cp % k
    e1 = jnp.zeros_like(cp)
    e2 = jnp.zeros_like(cp)
    for a in range(k):
        e1 = jnp.where(ca == a, si[0][a:a + 1], e1)
        e2 = jnp.where(cb == a, si[1][a:a + 1], e2)
    idx_ref[...] = e1 * n_keys + e2
    e = jnp.exp(cv - jnp.max(cv, axis=0, keepdims=True))
    g_ref[...] = e / jnp.sum(e, axis=0, keepdims=True)


def _peer_topk(q, subkeys, l):
    n = q.shape[0]
    depth, heads, _, n_keys, key_dim = subkeys.shape
    tq = 256
    k = PEER_TOPK
    kern = functools.partial(_peer_topk_kernel, n_keys=n_keys, key_dim=key_dim)
    idx, g = pl.pallas_call(
        kern,
        grid=(n // tq, heads),
        in_specs=[
            pl.BlockSpec((tq, 2 * key_dim), lambda i, h: (i, h)),
            pl.BlockSpec((None, None, 2, n_keys, key_dim), lambda i, h: (l, h, 0, 0, 0)),
        ],
        out_specs=[pl.BlockSpec((None, k, tq), lambda i, h: (h, 0, i))] * 2,
        out_shape=[jax.ShapeDtypeStruct((heads, k, n), jnp.int32),
                   jax.ShapeDtypeStruct((heads, k, n), F32)],
        compiler_params=_cparams(("arbitrary", "arbitrary")),
        name="peer_topk",
    )(q, subkeys)
    return idx.reshape(heads * k, n), g.reshape(heads * k, n)


PEER_TOKENS = 256
PEER_SLOTS = 8
PEER_DTYPE = BF16


def _expert_table(u, v):
    depth, e, d = u.shape
    uv = jnp.concatenate([u, v], axis=-1).astype(PEER_DTYPE)
    return uv.reshape(depth, e, 2 * d // LANES, LANES)


def _peer_kernel(idx_ref, g_ref, x_ref, res_ref, mod_ref, uv_hbm, o_ref, *scratch,
                 l, kk, n_ctx, dec_seq, gate_idx):
    tb, ns = PEER_TOKENS, PEER_SLOTS
    bufs = scratch[:ns]
    hbuf, abuf, xs, ys, sem = scratch[ns:]
    nc = xs.shape[0]
    nv = nc // SUBLANES
    n_groups = kk // SUBLANES
    step = pl.program_id(0)
    gate = mod_ref[_mod_index(step * tb, n_ctx, dec_seq), pl.ds(gate_idx, 1), :]
    sub = lax.broadcasted_iota(jnp.int32, (SUBLANES, LANES), 0)
    lane = lax.broadcasted_iota(jnp.int32, (SUBLANES, LANES), 1)

    def issue_group(t, slot, kg, js=range(SUBLANES)):
        base = t * kk + kg * SUBLANES
        for j in js:
            e = idx_ref[base + j]
            pltpu.make_async_copy(uv_hbm.at[l, e], bufs[slot].at[kg * SUBLANES + j],
                                  sem.at[slot]).start(priority=j % 2)

    def wait_slot(slot):
        pltpu.make_async_copy(uv_hbm.at[l, pl.ds(0, kk)], bufs[slot], sem.at[slot]).wait()

    def fold_sublanes(parts):
        dist = SUBLANES // 2
        while len(parts) > 1:
            low = (sub & dist) == 0
            half = len(parts) // 2
            nxt = []
            for i in range(half):
                keep = jnp.where(low, parts[i], parts[i + half])
                move = jnp.where(low, parts[i + half], parts[i])
                up = pltpu.roll(move, SUBLANES - dist, 0)
                swapped = up if 2 * dist == SUBLANES else jnp.where(low, up, pltpu.roll(move, dist, 0))
                nxt.append(keep + swapped)
            parts = nxt
            dist //= 2
        return parts[0]

    def pre_activations(t, slot, fetch):
        buf = bufs[slot]
        xrow = x_ref[pl.ds(t, 1), :]
        for c in range(nc):
            xs[c:c + 1, :] = xrow[:, c * LANES:(c + 1) * LANES]
        xv = [xs[i * SUBLANES:(i + 1) * SUBLANES, :] for i in range(nv)]
        hmat = jnp.zeros((SUBLANES, LANES), F32)
        for kg in range(n_groups):
            fetch(kg)
            parts = []
            for j in range(SUBLANES):
                uk = buf[kg * SUBLANES + j, 0:nc, :].astype(F32)
                s = uk[0:SUBLANES] * xv[0]
                for i in range(1, nv):
                    s = s + uk[i * SUBLANES:(i + 1) * SUBLANES] * xv[i]
                parts.append(s)
            hk = jnp.sum(fold_sublanes(parts), axis=-1, keepdims=True)
            hmat = jnp.where(lane == kg, hk, hmat)
        hbuf[...] = hmat

    def activations(t):
        hmat = hbuf[...]
        amat = g_ref[t] * (0.5 * hmat * (1.0 + lax.erf(hmat * np.float32(0.5 ** 0.5))))
        for kg in range(n_groups):
            abuf[kg * SUBLANES:(kg + 1) * SUBLANES, :] = jnp.broadcast_to(amat[:, kg:kg + 1], (SUBLANES, LANES))

    def combine(t, slot, fetch):
        buf = bufs[slot]
        acc = [jnp.zeros((SUBLANES, LANES), F32)] * (2 * nv)
        for k in range(kk):
            if k % SUBLANES == 0:
                fetch(k // SUBLANES)
            ak = jnp.broadcast_to(abuf[k:k + 1, :], (SUBLANES, LANES))
            vk = buf[k, nc:2 * nc, :].astype(F32)
            for i in range(nv):
                n = (k % 2) * nv + i
                acc[n] = acc[n] + vk[i * SUBLANES:(i + 1) * SUBLANES] * ak
        for i in range(nv):
            ys[i * SUBLANES:(i + 1) * SUBLANES, :] = acc[i] + acc[nv + i]
        y = jnp.concatenate([ys[c:c + 1, :] for c in range(nc)], axis=1)
        o_ref[pl.ds(t, 1), :] = res_ref[pl.ds(t, 1), :] + gate * y

    def stage(s, j, do_issue, do_act):
        half = SUBLANES // 2
        nothing = lambda kg: None
        fetch_lo = fetch_hi = nothing
        if do_issue:
            fetch_lo = lambda kg: issue_group(s + ns - 1, (j + ns - 1) % ns, kg, range(half))
            fetch_hi = lambda kg: issue_group(s + ns - 1, (j + ns - 1) % ns, kg, range(half, SUBLANES))
        if do_act:
            wait_slot((j + 1) % ns)
        activations(s)
        if do_act:
            pre_activations(s + 1, (j + 1) % ns, fetch_lo)
        else:
            for kg in range(n_groups):
                fetch_lo(kg)
        combine(s, j, fetch_hi)

    for t in range(ns - 1):
        lax.fori_loop(0, n_groups, lambda kg, c, t=t: (issue_group(t, t, kg), c)[1], 0)
    wait_slot(0)
    pre_activations(0, 0, lambda kg: None)
    n_rounds = tb // ns

    def full_round(r, c):
        for j in range(ns):
            stage(r * ns + j, j, True, True)
        return c
    lax.fori_loop(0, n_rounds - 1, full_round, 0)
    for j in range(ns):
        stage((n_rounds - 1) * ns + j, j, j == 0, j < ns - 1)


def _peer(idx, g, x, res, mod, uv, l, gate_idx, n_ctx, dec_seq):
    n, d = x.shape
    kk = idx.shape[1]
    r = mod.shape[1]
    nc = d // LANES
    tb, ns = PEER_TOKENS, PEER_SLOTS
    n_groups = kk // SUBLANES
    assert kk % SUBLANES == 0 and n_groups <= LANES and n % tb == 0 and tb % ns == 0 and ns % 2 == 0
    assert nc % (2 * SUBLANES) == 0
    g_tiles = jnp.pad(g.reshape(n_groups, SUBLANES, n).transpose(2, 1, 0),
                      ((0, 0), (0, 0), (0, LANES - n_groups)))
    kern = functools.partial(_peer_kernel, l=l, kk=kk, n_ctx=n_ctx, dec_seq=dec_seq, gate_idx=gate_idx)
    return pl.pallas_call(
        kern,
        grid=(n // tb,),
        in_specs=[
            pl.BlockSpec((tb * kk,), lambda i: (i,), memory_space=pltpu.SMEM),
            pl.BlockSpec((tb, SUBLANES, LANES), lambda i: (i, 0, 0)),
            pl.BlockSpec((tb, d), lambda i: (i, 0)),
            pl.BlockSpec((tb, d), lambda i: (i, 0)),
            pl.BlockSpec((None, r, 6, d), lambda i: (l, 0, 0, 0)),
            pl.BlockSpec(memory_space=pl.ANY),
        ],
        out_specs=pl.BlockSpec((tb, d), lambda i: (i, 0)),
        out_shape=jax.ShapeDtypeStruct((n, d), F32),
        scratch_shapes=[pltpu.VMEM((kk, 2 * nc, LANES), uv.dtype) for _ in range(ns)] + [
            pltpu.VMEM((SUBLANES, LANES), F32),
            pltpu.VMEM((kk, LANES), F32),
            pltpu.VMEM((nc, LANES), F32),
            pltpu.VMEM((nc, LANES), F32),
            pltpu.SemaphoreType.DMA((ns,)),
        ],
        compiler_params=_cparams(("arbitrary",)),
        name="peer_experts",
    )(idx.reshape(n * kk), g_tiles, x, res, mod, uv)


def kernel(x_prompt, x_sample, cache_na_k, cache_na_v, cache_swa_k, cache_swa_v, c, c_ctx,
           w_mod, b_mod, norm1_g, w_in, rpb, sink, out_norm_a, out_norm_b, w_out, norm2_g,
           peer_wq, peer_subkeys, peer_u, peer_v, final_g):
    batch, seq, d = x_prompt.shape
    dec_batch, dec_seq, _ = x_sample.shape
    depth = w_mod.shape[0]
    past = cache_na_k.shape[2]
    na_heads = cache_na_k.shape[3]
    swa_kv = cache_swa_k.shape[3]
    swa_heads = sink.shape[1]
    group = swa_heads // swa_kv
    n_ctx = batch * seq
    scale = HEAD_DIM ** -0.5
    qa_col, ka_col, va_col = 0, na_heads, 2 * na_heads
    qb_col = 3 * na_heads
    kb_col = qb_col + swa_heads
    vb_col = kb_col + swa_kv

    x = jnp.concatenate([x_prompt.reshape(n_ctx, d), x_sample.reshape(dec_batch * dec_seq, d)], axis=0)

    n_mod = 1 + dec_batch
    r_mod = -(-n_mod // SUBLANES) * SUBLANES
    cond = jnp.concatenate([c_ctx[None, :], c, jnp.zeros((r_mod - n_mod, d), F32)], axis=0)
    mod = _adaln(cond, w_mod, b_mod).reshape(depth, r_mod, 6, d)

    cos, sin = _rope_tables(dec_seq)
    na_bias = _na_bias(rpb, dec_seq // GRID_W)
    uv_tab = _expert_table(peer_u, peer_v)
    na_k, na_v, swa_k, swa_v = [], [], [], []
    for l in range(depth):
        (h,) = _norm_mod(x, norm1_g, mod, l, 0, 1, n_ctx, dec_seq, [BF16])
        p = _matmul(h, w_in, l, n_ctx, dec_seq)
        oa_ctx = _ctx_attn(p, None, batch=batch, seq=seq, n_kv=na_heads, group=1,
                           q_col=qa_col, k_col=ka_col, v_col=va_col, scale=scale)
        ob_ctx = _ctx_attn(p, sink[l], batch=batch, seq=seq, n_kv=swa_kv, group=group,
                           q_col=qb_col, k_col=kb_col, v_col=vb_col, scale=scale)
        oa_lat = _na_lat(p, cache_na_k, cache_na_v, na_bias, l,
                         n_ctx=n_ctx, batch=dec_batch, t=dec_seq, heads=na_heads, past=past, scale=scale)
        ob_lat = _swa_lat(p, cache_swa_k, cache_swa_v, sink[l], cos, sin, l,
                          n_ctx=n_ctx, batch=dec_batch, t=dec_seq, n_kv=swa_kv, group=group, past=past,
                          q_col=qb_col, k_col=kb_col, v_col=vb_col, scale=scale)
        hn = _group_norm(oa_ctx, ob_ctx, oa_lat, ob_lat, out_norm_a, out_norm_b, l)
        x = _matmul_residual(hn, w_out, x, mod, l, 2, n_ctx, dec_seq)

        h2_bf, h2 = _norm_mod(x, norm2_g, mod, l, 3, 4, n_ctx, dec_seq, [BF16, F32])
        q = _matmul(h2_bf, peer_wq, l, n_ctx, dec_seq)
        idx_t, g_t = _peer_topk(q, peer_subkeys, l)
        x = _peer(idx_t.T, g_t, h2, x, mod, uv_tab, l, 5, n_ctx, dec_seq)

        pc = p[:n_ctx]
        na_k.append(pc[:, ka_col * HEAD_DIM:va_col * HEAD_DIM].reshape(batch, seq, na_heads, HEAD_DIM))
        na_v.append(pc[:, va_col * HEAD_DIM:qb_col * HEAD_DIM].reshape(batch, seq, na_heads, HEAD_DIM))
        swa_k.append(pc[:, kb_col * HEAD_DIM:vb_col * HEAD_DIM].reshape(batch, seq, swa_kv, HEAD_DIM))
        swa_v.append(pc[:, vb_col * HEAD_DIM:].reshape(batch, seq, swa_kv, HEAD_DIM))

    y = _final_norm(x, final_g)
    return (y[:n_ctx].reshape(batch, seq, d), y[n_ctx:].reshape(dec_batch, dec_seq, d),
            jnp.stack(na_k, axis=1), jnp.stack(na_v, axis=1),
            jnp.stack(swa_k, axis=1), jnp.stack(swa_v, axis=1))
```

```python
import functools

import jax
import jax.numpy as jnp
import numpy as np
from jax import lax
from jax.experimental import pallas as pl
from jax.experimental.pallas import tpu as pltpu

GRID_W = 64
NA_ROWS = 8
NA_COLS = 16
SWA_WINDOW = 128
SWA_BLOCK = 128
ROPE_BASE = 10000.0
PEER_TOPK = 16
EPS = 1e-6
NEG_INF = -1e30

HEAD_DIM = 128
LANES = 128
SUBLANES = 8
VMEM_LIMIT = 56 * 1024 * 1024

F32 = jnp.float32
BF16 = jnp.bfloat16


def _cparams(sem):
    return pltpu.CompilerParams(dimension_semantics=sem, vmem_limit_bytes=VMEM_LIMIT)


def _mod_index(row0, n_ctx, dec_seq):
    return jnp.where(row0 < n_ctx, 0, 1 + (row0 - n_ctx) // dec_seq)


def _adaln_kernel(c_ref, w_ref, b_ref, o_ref):
    c = c_ref[...]
    s = (c * jax.nn.sigmoid(c)).astype(BF16)
    o_ref[...] = jnp.dot(s, w_ref[...].astype(BF16), preferred_element_type=F32) + b_ref[...]


def _adaln(cond, w_mod, b_mod):
    depth, d, n = w_mod.shape
    r = cond.shape[0]
    tn = next(t for t in (1024, 512, 256, LANES) if n % t == 0)
    return pl.pallas_call(
        _adaln_kernel,
        grid=(depth, n // tn),
        in_specs=[
            pl.BlockSpec((r, d), lambda l, j: (0, 0)),
            pl.BlockSpec((None, d, tn), lambda l, j: (l, 0, j)),
            pl.BlockSpec((None, 1, tn), lambda l, j: (l, 0, j)),
        ],
        out_specs=pl.BlockSpec((None, r, tn), lambda l, j: (l, 0, j)),
        out_shape=jax.ShapeDtypeStruct((depth, r, n), F32),
        compiler_params=_cparams(("arbitrary", "arbitrary")),
        name="adaln",
    )(cond, w_mod, b_mod.reshape(depth, 1, n))


def _norm_mod_kernel(x_ref, g_ref, mod_ref, *o_refs, tr, n_ctx, dec_seq, shift_idx, scale_idx):
    x = x_ref[...]
    y = x * lax.rsqrt(jnp.mean(x * x, axis=-1, keepdims=True) + EPS) * g_ref[...]
    m = _mod_index(pl.program_id(0) * tr, n_ctx, dec_seq)
    shift = mod_ref[m, pl.ds(shift_idx, 1), :]
    scale = mod_ref[m, pl.ds(scale_idx, 1), :]
    h = y * (1.0 + scale) + shift
    for o_ref in o_refs:
        o_ref[...] = h.astype(o_ref.dtype)


def _norm_mod(x, g, mod, l, shift_idx, scale_idx, n_ctx, dec_seq, out_dtypes):
    n, d = x.shape
    tr = 256
    depth, r = mod.shape[:2]
    kern = functools.partial(_norm_mod_kernel, tr=tr, n_ctx=n_ctx, dec_seq=dec_seq,
                             shift_idx=shift_idx, scale_idx=scale_idx)
    outs = pl.pallas_call(
        kern,
        grid=(n // tr,),
        in_specs=[
            pl.BlockSpec((tr, d), lambda i: (i, 0)),
            pl.BlockSpec((None, 1, d), lambda i: (l, 0, 0)),
            pl.BlockSpec((None, r, 6, d), lambda i: (l, 0, 0, 0)),
        ],
        out_specs=[pl.BlockSpec((tr, d), lambda i: (i, 0)) for _ in out_dtypes],
        out_shape=[jax.ShapeDtypeStruct((n, d), dt) for dt in out_dtypes],
        compiler_params=_cparams(("arbitrary",)),
        name="norm_mod",
    )(x, g.reshape(depth, 1, d), mod)
    return outs


def _final_norm_kernel(x_ref, g_ref, o_ref):
    x = x_ref[...]
    o_ref[...] = x * lax.rsqrt(jnp.mean(x * x, axis=-1, keepdims=True) + EPS) * g_ref[...]


def _final_norm(x, g):
    n, d = x.shape
    tr = 256
    return pl.pallas_call(
        _final_norm_kernel,
        grid=(n // tr,),
        in_specs=[pl.BlockSpec((tr, d), lambda i: (i, 0)), pl.BlockSpec((1, d), lambda i: (0, 0))],
        out_specs=pl.BlockSpec((tr, d), lambda i: (i, 0)),
        out_shape=jax.ShapeDtypeStruct((n, d), F32),
        compiler_params=_cparams(("arbitrary",)),
        name="final_norm",
    )(x, g.reshape(1, d))


def _group_norm_kernel(oac_ref, obc_ref, oal_ref, obl_ref, ga_ref, gb_ref, h_ref, *, mix_a, ctx_tiles):
    def nrm(x, g):
        return (x * lax.rsqrt(jnp.mean(x * x, axis=-1, keepdims=True) + EPS) * g).astype(h_ref.dtype)

    def emit(oa_ref, ob_ref):
        h_ref[:, :mix_a] = nrm(oa_ref[...], ga_ref[...])
        h_ref[:, mix_a:] = nrm(ob_ref[...], gb_ref[...])

    is_ctx = pl.program_id(0) < ctx_tiles
    pl.when(is_ctx)(lambda: emit(oac_ref, obc_ref))
    pl.when(jnp.logical_not(is_ctx))(lambda: emit(oal_ref, obl_ref))


def _group_norm(oa_ctx, ob_ctx, oa_lat, ob_lat, ga, gb, l):
    n_ctx, mix_a = oa_ctx.shape
    n_lat, mix_b = ob_lat.shape
    depth = ga.shape[0]
    tr = 256
    ctx_tiles = n_ctx // tr
    lat_tiles = n_lat // tr
    ctx_map = lambda i: (jnp.minimum(i, ctx_tiles - 1), 0)
    lat_map = lambda i: (jnp.maximum(i - ctx_tiles, 0), 0)
    return pl.pallas_call(
        functools.partial(_group_norm_kernel, mix_a=mix_a, ctx_tiles=ctx_tiles),
        grid=(ctx_tiles + lat_tiles,),
        in_specs=[
            pl.BlockSpec((tr, mix_a), ctx_map),
            pl.BlockSpec((tr, mix_b), ctx_map),
            pl.BlockSpec((tr, mix_a), lat_map),
            pl.BlockSpec((tr, mix_b), lat_map),
            pl.BlockSpec((None, 1, mix_a), lambda i: (l, 0, 0)),
            pl.BlockSpec((None, 1, mix_b), lambda i: (l, 0, 0)),
        ],
        out_specs=pl.BlockSpec((tr, mix_a + mix_b), lambda i: (i, 0)),
        out_shape=jax.ShapeDtypeStruct((n_ctx + n_lat, mix_a + mix_b), BF16),
        compiler_params=_cparams(("arbitrary",)),
        name="group_norm",
    )(oa_ctx, ob_ctx, oa_lat, ob_lat, ga.reshape(depth, 1, mix_a), gb.reshape(depth, 1, mix_b))


def _mm_kernel(a_ref, w_ref, o_ref):
    o_ref[...] = jnp.dot(a_ref[...], w_ref[...].astype(BF16), preferred_element_type=F32)


def _mm_res_kernel(a_ref, w_ref, x_ref, mod_ref, o_ref, *, tm, n_ctx, dec_seq, gate_idx):
    acc = jnp.dot(a_ref[...], w_ref[...].astype(BF16), preferred_element_type=F32)
    m = _mod_index(pl.program_id(0) * tm, n_ctx, dec_seq)
    gate = mod_ref[m, pl.ds(gate_idx, 1), :]
    o_ref[...] = x_ref[...] + gate * acc


def _row_tile(n_ctx, dec_seq):
    tm = min(1024, dec_seq)
    assert n_ctx % tm == 0 and dec_seq % tm == 0
    return tm


def _col_tile(m):
    return next(t for t in (512, 256, LANES) if m % t == 0)


def _matmul(a, w, l, n_ctx, dec_seq):
    n, k = a.shape
    m = w.shape[2]
    tm = _row_tile(n_ctx, dec_seq)
    tn = _col_tile(m)
    return pl.pallas_call(
        _mm_kernel,
        grid=(n // tm, m // tn),
        in_specs=[
            pl.BlockSpec((tm, k), lambda i, j: (i, 0)),
            pl.BlockSpec((None, k, tn), lambda i, j: (l, 0, j)),
        ],
        out_specs=pl.BlockSpec((tm, tn), lambda i, j: (i, j)),
        out_shape=jax.ShapeDtypeStruct((n, m), F32),
        compiler_params=_cparams(("arbitrary", "arbitrary")),
        name="matmul",
    )(a, w)


def _matmul_residual(a, w, x, mod, l, gate_idx, n_ctx, dec_seq):
    n, k = a.shape
    m = w.shape[2]
    r = mod.shape[1]
    tm = _row_tile(n_ctx, dec_seq)
    tn = _col_tile(m)
    kern = functools.partial(_mm_res_kernel, tm=tm, n_ctx=n_ctx, dec_seq=dec_seq, gate_idx=gate_idx)
    return pl.pallas_call(
        kern,
        grid=(n // tm, m // tn),
        in_specs=[
            pl.BlockSpec((tm, k), lambda i, j: (i, 0)),
            pl.BlockSpec((None, k, tn), lambda i, j: (l, 0, j)),
            pl.BlockSpec((tm, tn), lambda i, j: (i, j)),
            pl.BlockSpec((None, r, 6, tn), lambda i, j: (l, 0, 0, j)),
        ],
        out_specs=pl.BlockSpec((tm, tn), lambda i, j: (i, j)),
        out_shape=jax.ShapeDtypeStruct((n, m), F32),
        compiler_params=_cparams(("arbitrary", "arbitrary")),
        name="matmul_residual",
    )(a, w, x, mod)


_NT = (((1,), (1,)), ((), ()))


def _qk(q, k):
    return lax.dot_general(q, k, _NT, preferred_element_type=F32)


def _pv(p, v):
    return jnp.dot(p.astype(BF16), v, preferred_element_type=F32)


def _softmax_pv(score_blocks, value_blocks, sink=None):
    m = functools.reduce(jnp.maximum, [jnp.max(s, axis=-1, keepdims=True) for s in score_blocks])
    if sink is not None:
        m = jnp.maximum(m, sink)
    ps = [jnp.exp(s - m) for s in score_blocks]
    denom = functools.reduce(jnp.add, [jnp.sum(p, axis=-1, keepdims=True) for p in ps])
    if sink is not None:
        denom = denom + jnp.exp(sink - m)
    o = functools.reduce(jnp.add, [_pv(p, v) for p, v in zip(ps, value_blocks)])
    return o / denom


def _ctx_attn_kernel(*refs, group, scale, has_sink):
    if has_sink:
        sink_ref, q_ref, k_ref, v_ref, o_ref = refs
    else:
        q_ref, k_ref, v_ref, o_ref = refs
    k = k_ref[...].astype(BF16)
    v = v_ref[...].astype(BF16)
    kv = pl.program_id(1)
    for g in range(group):
        q = q_ref[:, g * HEAD_DIM:(g + 1) * HEAD_DIM].astype(BF16)
        s = _qk(q, k) * scale
        sink = sink_ref[kv * group + g] if has_sink else None
        o_ref[:, g * HEAD_DIM:(g + 1) * HEAD_DIM] = _softmax_pv([s], [v], sink)


def _ctx_attn(p, sink_l, *, batch, seq, n_kv, group, q_col, k_col, v_col, scale):
    gw = group * HEAD_DIM
    has_sink = sink_l is not None
    kern = functools.partial(_ctx_attn_kernel, group=group, scale=scale, has_sink=has_sink)
    in_specs = [
        pl.BlockSpec((seq, gw), lambda b, h: (b, q_col // group + h)),
        pl.BlockSpec((seq, HEAD_DIM), lambda b, h: (b, k_col + h)),
        pl.BlockSpec((seq, HEAD_DIM), lambda b, h: (b, v_col + h)),
    ]
    args = [p, p, p]
    if has_sink:
        in_specs = [pl.BlockSpec(memory_space=pltpu.SMEM)] + in_specs
        args = [sink_l] + args
    return pl.pallas_call(
        kern,
        grid=(batch, n_kv),
        in_specs=in_specs,
        out_specs=pl.BlockSpec((seq, gw), lambda b, h: (b, h)),
        out_shape=jax.ShapeDtypeStruct((batch * seq, n_kv * gw), F32),
        compiler_params=_cparams(("arbitrary", "arbitrary")),
        name="ctx_attn_sink" if has_sink else "ctx_attn",
    )(*args)


def _na_groups(rows):
    kr = min(NA_ROWS, rows)
    starts = np.clip(np.arange(rows) - kr // 2, 0, rows - kr)
    groups, r = [], 0
    while r < rows:
        e = r
        while e + 1 < rows and starts[e + 1] == starts[r]:
            e += 1
        groups.append((r, e - r + 1, int(starts[r])))
        r = e + 1
    return kr, groups


def _na_bias(rpb, rows):
    depth, heads = rpb.shape[:2]
    kr = min(NA_ROWS, rows)
    r = np.arange(rows)
    d0 = np.clip(r - kr // 2, 0, rows - kr) - r + (NA_ROWS - 1)
    by_row = jnp.stack([rpb[:, :, int(s):int(s) + kr, :] for s in d0], axis=2)
    cq = np.arange(GRID_W)
    cs = np.clip(cq - NA_COLS // 2, 0, GRID_W - NA_COLS)
    col_ok = (cq[None, :] >= cs[:, None]) & (cq[None, :] < cs[:, None] + NA_COLS)
    dc = np.clip(cq[None, :] - cq[:, None] + (NA_COLS - 1), 0, 2 * NA_COLS - 2)
    onehot = (dc[None, :, :] == np.arange(2 * NA_COLS - 1)[:, None, None]).astype(np.float32)
    bias = jnp.einsum('lhrij,jqk->lhrqik', by_row.astype(F32), onehot, precision=lax.Precision.HIGHEST)
    bias = jnp.where(col_ok[None, None, None, :, None, :], bias, NEG_INF)
    return bias.reshape(depth, heads, rows * GRID_W, kr * GRID_W)


def _na_lat_kernel(q_ref, k_ref, v_ref, kc_ref, vc_ref, bias_ref, o_ref, *, groups, kr, scale):
    q = q_ref[...].astype(BF16)
    k = k_ref[...].astype(BF16)
    v = v_ref[...].astype(BF16)
    kc = kc_ref[...].astype(BF16)
    vc = vc_ref[...].astype(BF16)
    for r0, nr, kr0 in groups:
        q0, q1 = r0 * GRID_W, (r0 + nr) * GRID_W
        k0, k1 = kr0 * GRID_W, (kr0 + kr) * GRID_W
        qg = q[q0:q1]
        s_lat = _qk(qg, k[k0:k1]) * scale + bias_ref[q0:q1, :]
        s_ctx = _qk(qg, kc) * scale
        o_ref[q0:q1, :] = _softmax_pv([s_lat, s_ctx], [v[k0:k1], vc])


def _na_lat(p, cache_k, cache_v, bias, l, *, n_ctx, batch, t, heads, past, scale):
    rows = t // GRID_W
    kr, groups = _na_groups(rows)
    rb = n_ctx // t
    depth = cache_k.shape[1]
    ck = cache_k.reshape(batch, depth, past, heads * HEAD_DIM)
    cv = cache_v.reshape(batch, depth, past, heads * HEAD_DIM)
    kern = functools.partial(_na_lat_kernel, groups=groups, kr=kr, scale=scale)
    return pl.pallas_call(
        kern,
        grid=(heads, batch),
        in_specs=[
            pl.BlockSpec((t, HEAD_DIM), lambda h, b: (rb + b, h)),
            pl.BlockSpec((t, HEAD_DIM), lambda h, b: (rb + b, heads + h)),
            pl.BlockSpec((t, HEAD_DIM), lambda h, b: (rb + b, 2 * heads + h)),
            pl.BlockSpec((None, None, past, HEAD_DIM), lambda h, b: (b, l, 0, h)),
            pl.BlockSpec((None, None, past, HEAD_DIM), lambda h, b: (b, l, 0, h)),
            pl.BlockSpec((None, None, t, kr * GRID_W), lambda h, b: (l, h, 0, 0)),
        ],
        out_specs=pl.BlockSpec((t, HEAD_DIM), lambda h, b: (b, h)),
        out_shape=jax.ShapeDtypeStruct((batch * t, heads * HEAD_DIM), F32),
        compiler_params=_cparams(("arbitrary", "arbitrary")),
        name="na_latent",
    )(p, p, p, ck, cv, bias)


def _rope_tables(t):
    pos = np.arange(t)
    nf = HEAD_DIM // 4
    inv = jnp.asarray(ROPE_BASE, F32) ** (-jnp.arange(nf, dtype=F32) / nf)
    ang_r = jnp.asarray(pos // GRID_W, F32)[:, None] * inv
    ang_c = jnp.asarray(pos % GRID_W, F32)[:, None] * inv
    cos = jnp.concatenate([jnp.cos(ang_r)] * 2 + [jnp.cos(ang_c)] * 2, axis=-1)
    sin = jnp.concatenate([-jnp.sin(ang_r), jnp.sin(ang_r), -jnp.sin(ang_c), jnp.sin(ang_c)], axis=-1)
    return cos, sin


def _rope(x, cos, sin):
    nf = HEAD_DIM // 4
    lane = lax.broadcasted_iota(jnp.int32, x.shape, 1)
    swapped = jnp.where(lane % (2 * nf) < nf,
                        pltpu.roll(x, HEAD_DIM - nf, 1),
                        pltpu.roll(x, nf, 1))
    return x * cos + swapped * sin


def _swa_lat_kernel(sink_ref, q_ref, k_ref, v_ref, kc_ref, vc_ref, cos_ref, sin_ref, o_ref,
                    krot_ref, vbf_ref, *, group, t, scale):
    kv = pl.program_id(1)
    krot_ref[...] = _rope(k_ref[...], cos_ref[...], sin_ref[...]).astype(BF16)
    vbf_ref[...] = v_ref[...].astype(BF16)
    kc = kc_ref[...].astype(BF16)
    vc = vc_ref[...].astype(BF16)
    span = 3 * SWA_BLOCK

    def block(n, carry):
        q0 = pl.multiple_of(n * SWA_BLOCK, SWA_BLOCK)
        w0 = pl.multiple_of(jnp.clip((n - 1) * SWA_BLOCK, 0, t - span), SWA_BLOCK)
        kw = krot_ref[pl.ds(w0, span), :]
        vw = vbf_ref[pl.ds(w0, span), :]
        qpos = q0 + lax.broadcasted_iota(jnp.int32, (SWA_BLOCK, span), 0)
        kpos = w0 + lax.broadcasted_iota(jnp.int32, (SWA_BLOCK, span), 1)
        valid = jnp.abs(kpos - qpos) <= SWA_WINDOW
        cos = cos_ref[pl.ds(q0, SWA_BLOCK), :]
        sin = sin_ref[pl.ds(q0, SWA_BLOCK), :]
        for g in range(group):
            cols = slice(g * HEAD_DIM, (g + 1) * HEAD_DIM)
            qg = _rope(q_ref[pl.ds(q0, SWA_BLOCK), cols], cos, sin).astype(BF16)
            s_lat = jnp.where(valid, _qk(qg, kw) * scale, NEG_INF)
            s_ctx = _qk(qg, kc) * scale
            o_ref[pl.ds(q0, SWA_BLOCK), cols] = _softmax_pv([s_lat, s_ctx], [vw, vc],
                                                           sink_ref[kv * group + g])
        return carry

    lax.fori_loop(0, t // SWA_BLOCK, block, 0)


def _swa_lat(p, cache_k, cache_v, sink_l, cos, sin, l, *, n_ctx, batch, t, n_kv, group, past,
             q_col, k_col, v_col, scale):
    assert t >= 3 * SWA_BLOCK and t % SWA_BLOCK == 0
    gw = group * HEAD_DIM
    rb = n_ctx // t
    depth = cache_k.shape[1]
    ck = cache_k.reshape(batch, depth, past, n_kv * HEAD_DIM)
    cv = cache_v.reshape(batch, depth, past, n_kv * HEAD_DIM)
    kern = functools.partial(_swa_lat_kernel, group=group, t=t, scale=scale)
    return pl.pallas_call(
        kern,
        grid=(batch, n_kv),
        in_specs=[
            pl.BlockSpec(memory_space=pltpu.SMEM),
            pl.BlockSpec((t, gw), lambda b, h: (rb + b, q_col // group + h)),
            pl.BlockSpec((t, HEAD_DIM), lambda b, h: (rb + b, k_col + h)),
            pl.BlockSpec((t, HEAD_DIM), lambda b, h: (rb + b, v_col + h)),
            pl.BlockSpec((None, None, past, HEAD_DIM), lambda b, h: (b, l, 0, h)),
            pl.BlockSpec((None, None, past, HEAD_DIM), lambda b, h: (b, l, 0, h)),
            pl.BlockSpec((t, HEAD_DIM), lambda b, h: (0, 0)),
            pl.BlockSpec((t, HEAD_DIM), lambda b, h: (0, 0)),
        ],
        out_specs=pl.BlockSpec((t, gw), lambda b, h: (b, h)),
        out_shape=jax.ShapeDtypeStruct((batch * t, n_kv * gw), F32),
        scratch_shapes=[pltpu.VMEM((t, HEAD_DIM), BF16), pltpu.VMEM((t, HEAD_DIM), BF16)],
        compiler_params=_cparams(("arbitrary", "arbitrary")),
        name="swa_latent",
    )(sink_l, p, p, p, ck, cv, cos, sin)


def _topk_rows(v, k):
    n = v.shape[0]
    iota = lax.broadcasted_iota(jnp.int32, v.shape, 0)
    vals, idxs = [], []
    for _ in range(k):
        m = jnp.max(v, axis=0, keepdims=True)
        first = jnp.min(jnp.where(v == m, iota, n), axis=0, keepdims=True)
        v = jnp.where(iota == first, -jnp.inf, v)
        vals.append(m)
        idxs.append(first)
    return jnp.concatenate(vals, axis=0), jnp.concatenate(idxs, axis=0)


def _peer_topk_kernel(q_ref, sk_ref, idx_ref, g_ref, *, n_keys, key_dim):
    k = PEER_TOPK
    sv, si = [], []
    for part in range(2):
        q = q_ref[:, part * key_dim:(part + 1) * key_dim].astype(BF16)
        s = _qk(sk_ref[part].astype(BF16), q)
        v, i = _topk_rows(s, k)
        sv.append(v)
        si.append(i)
    cand = jnp.concatenate([sv[0][a:a + 1] + sv[1] for a in range(k)], axis=0)
    cv, cp = _topk_rows(cand, k)
    ca, cb = cp // k, cp % k
    e1 = jnp.zeros_like(cp)
    e2 = jnp.zeros_like(cp)
    for a in range(k):
        e1 = jnp.where(ca == a, si[0][a:a + 1], e1)
        e2 = jnp.where(cb == a, si[1][a:a + 1], e2)
    idx_ref[...] = e1 * n_keys + e2
    e = jnp.exp(cv - jnp.max(cv, axis=0, keepdims=True))
    g_ref[...] = e / jnp.sum(e, axis=0, keepdims=True)


def _peer_topk(q, subkeys, l):
    n = q.shape[0]
    depth, heads, _, n_keys, key_dim = subkeys.shape
    tq = 256
    k = PEER_TOPK
    kern = functools.partial(_peer_topk_kernel, n_keys=n_keys, key_dim=key_dim)
    idx, g = pl.pallas_call(
        kern,
        grid=(n // tq, heads),
        in_specs=[
            pl.BlockSpec((tq, 2 * key_dim), lambda i, h: (i, h)),
            pl.BlockSpec((None, None, 2, n_keys, key_dim), lambda i, h: (l, h, 0, 0, 0)),
        ],
        out_specs=[pl.BlockSpec((None, k, tq), lambda i, h: (h, 0, i))] * 2,
        out_shape=[jax.ShapeDtypeStruct((heads, k, n), jnp.int32),
                   jax.ShapeDtypeStruct((heads, k, n), F32)],
        compiler_params=_cparams(("arbitrary", "arbitrary")),
        name="peer_topk",
    )(q, subkeys)
    return idx.reshape(heads * k, n), g.reshape(heads * k, n)


PEER_TOKENS = 128
PEER_SLOTS = 8
PEER_DTYPE = BF16


def _expert_table(u, v):
    depth, e, d = u.shape
    uv = jnp.concatenate([u, v], axis=-1).astype(PEER_DTYPE)
    return uv.reshape(depth, e, 2 * d // LANES, LANES)


def _peer_kernel(idx_ref, idxn_ref, g_ref, x_ref, res_ref, mod_ref, uv_hbm, o_ref, *scratch,
                 l, kk, n_ctx, dec_seq, gate_idx, n_steps):
    tb, ns = PEER_TOKENS, PEER_SLOTS
    bufs = scratch[:ns]
    hbuf, abuf, xs, ys, sem = scratch[ns:]
    nc = xs.shape[0]
    nv = nc // SUBLANES
    n_groups = kk // SUBLANES
    step = pl.program_id(0)
    gate = mod_ref[_mod_index(step * tb, n_ctx, dec_seq), pl.ds(gate_idx, 1), :]
    sub = lax.broadcasted_iota(jnp.int32, (SUBLANES, LANES), 0)
    lane = lax.broadcasted_iota(jnp.int32, (SUBLANES, LANES), 1)

    def issue_group(t, slot, kg, js=range(SUBLANES), table=idx_ref):
        base = t * kk + kg * SUBLANES
        for j in js:
            e = table[base + j]
            pltpu.make_async_copy(uv_hbm.at[l, e], bufs[slot].at[kg * SUBLANES + j],
                                  sem.at[slot]).start(priority=j % 2)

    def wait_slot(slot):
        pltpu.make_async_copy(uv_hbm.at[l, pl.ds(0, kk)], bufs[slot], sem.at[slot]).wait()

    def fold_sublanes(parts):
        dist = SUBLANES // 2
        while len(parts) > 1:
            low = (sub & dist) == 0
            half = len(parts) // 2
            nxt = []
            for i in range(half):
                keep = jnp.where(low, parts[i], parts[i + half])
                move = jnp.where(low, parts[i + half], parts[i])
                up = pltpu.roll(move, SUBLANES - dist, 0)
                swapped = up if 2 * dist == SUBLANES else jnp.where(low, up, pltpu.roll(move, dist, 0))
                nxt.append(keep + swapped)
            parts = nxt
            dist //= 2
        return parts[0]

    def pre_activations(t, slot, fetch):
        buf = bufs[slot]
        xrow = x_ref[pl.ds(t, 1), :]
        for c in range(nc):
            xs[c:c + 1, :] = xrow[:, c * LANES:(c + 1) * LANES]
        xv = [xs[i * SUBLANES:(i + 1) * SUBLANES, :] for i in range(nv)]
        hmat = jnp.zeros((SUBLANES, LANES), F32)
        for kg in range(n_groups):
            fetch(kg)
            parts = []
            for j in range(SUBLANES):
                uk = buf[kg * SUBLANES + j, 0:nc, :].astype(F32)
                s = uk[0:SUBLANES] * xv[0]
                for i in range(1, nv):
                    s = s + uk[i * SUBLANES:(i + 1) * SUBLANES] * xv[i]
                parts.append(s)
            hk = jnp.sum(fold_sublanes(parts), axis=-1, keepdims=True)
            hmat = jnp.where(lane == kg, hk, hmat)
        hbuf[...] = hmat

    def activations(t):
        hmat = hbuf[...]
        amat = g_ref[t] * (0.5 * hmat * (1.0 + lax.erf(hmat * np.float32(0.5 ** 0.5))))
        for kg in range(n_groups):
            abuf[kg * SUBLANES:(kg + 1) * SUBLANES, :] = jnp.broadcast_to(amat[:, kg:kg + 1], (SUBLANES, LANES))

    def combine(t, slot, fetch):
        buf = bufs[slot]
        acc = [jnp.zeros((SUBLANES, LANES), F32)] * (2 * nv)
        for k in range(kk):
            if k % SUBLANES == 0:
                fetch(k // SUBLANES)
            ak = jnp.broadcast_to(abuf[k:k + 1, :], (SUBLANES, LANES))
            vk = buf[k, nc:2 * nc, :].astype(F32)
            for i in range(nv):
                n = (k % 2) * nv + i
                acc[n] = acc[n] + vk[i * SUBLANES:(i + 1) * SUBLANES] * ak
        for i in range(nv):
            ys[i * SUBLANES:(i + 1) * SUBLANES, :] = acc[i] + acc[nv + i]
        y = jnp.concatenate([ys[c:c + 1, :] for c in range(nc)], axis=1)
        o_ref[pl.ds(t, 1), :] = res_ref[pl.ds(t, 1), :] + gate * y

    def stage(s, j, do_issue, do_act):
        half = SUBLANES // 2
        nothing = lambda kg: None
        fetch_lo = fetch_hi = nothing
        if do_issue:
            tok, table = (s + ns - 1 - tb, idxn_ref) if do_issue == "next" else (s + ns - 1, idx_ref)
            fetch_lo = lambda kg: issue_group(tok, (j + ns - 1) % ns, kg, range(half), table)
            fetch_hi = lambda kg: issue_group(tok, (j + ns - 1) % ns, kg, range(half, SUBLANES), table)
        if do_act:
            wait_slot((j + 1) % ns)
        activations(s)
        if do_act:
            pre_activations(s + 1, (j + 1) % ns, fetch_lo)
        else:
            for kg in range(n_groups):
                fetch_lo(kg)
        combine(s, j, fetch_hi)

    @pl.when(step == 0)
    def _():
        for t in range(ns - 1):
            lax.fori_loop(0, n_groups, lambda kg, c, t=t: (issue_group(t, t, kg), c)[1], 0)
    wait_slot(0)
    pre_activations(0, 0, lambda kg: None)
    n_rounds = tb // ns

    def full_round(r, c):
        for j in range(ns):
            stage(r * ns + j, j, True, True)
        return c
    lax.fori_loop(0, n_rounds - 1, full_round, 0)
    def last_round(ahead):
        for j in range(ns):
            stage((n_rounds - 1) * ns + j, j, True if j == 0 else ahead, j < ns - 1)
    pl.when(step < n_steps - 1)(lambda: last_round("next"))
    pl.when(step == n_steps - 1)(lambda: last_round(False))


def _peer(idx, g, x, res, mod, uv, l, gate_idx, n_ctx, dec_seq):
    n, d = x.shape
    kk = idx.shape[1]
    r = mod.shape[1]
    nc = d // LANES
    tb, ns = PEER_TOKENS, PEER_SLOTS
    n_groups = kk // SUBLANES
    assert kk % SUBLANES == 0 and n_groups <= LANES and n % tb == 0 and tb % ns == 0 and ns % 2 == 0
    assert nc % (2 * SUBLANES) == 0
    g_tiles = jnp.pad(g.reshape(n_groups, SUBLANES, n).transpose(2, 1, 0),
                      ((0, 0), (0, 0), (0, LANES - n_groups)))
    n_steps = n // tb
    kern = functools.partial(_peer_kernel, l=l, kk=kk, n_ctx=n_ctx, dec_seq=dec_seq, gate_idx=gate_idx,
                             n_steps=n_steps)
    return pl.pallas_call(
        kern,
        grid=(n_steps,),
        in_specs=[
            pl.BlockSpec((tb * kk,), lambda i: (i,), memory_space=pltpu.SMEM),
            pl.BlockSpec((tb * kk,), lambda i: (jnp.minimum(i + 1, n_steps - 1),), memory_space=pltpu.SMEM),
            pl.BlockSpec((tb, SUBLANES, LANES), lambda i: (i, 0, 0)),
            pl.BlockSpec((tb, d), lambda i: (i, 0)),
            pl.BlockSpec((tb, d), lambda i: (i, 0)),
            pl.BlockSpec((None, r, 6, d), lambda i: (l, 0, 0, 0)),
            pl.BlockSpec(memory_space=pl.ANY),
        ],
        out_specs=pl.BlockSpec((tb, d), lambda i: (i, 0)),
        out_shape=jax.ShapeDtypeStruct((n, d), F32),
        scratch_shapes=[pltpu.VMEM((kk, 2 * nc, LANES), uv.dtype) for _ in range(ns)] + [
            pltpu.VMEM((SUBLANES, LANES), F32),
            pltpu.VMEM((kk, LANES), F32),
            pltpu.VMEM((nc, LANES), F32),
            pltpu.VMEM((nc, LANES), F32),
            pltpu.SemaphoreType.DMA((ns,)),
        ],
        compiler_params=_cparams(("arbitrary",)),
        name="peer_experts",
    )(idx.reshape(n * kk), idx.reshape(n * kk), g_tiles, x, res, mod, uv)


def kernel(x_prompt, x_sample, cache_na_k, cache_na_v, cache_swa_k, cache_swa_v, c, c_ctx,
           w_mod, b_mod, norm1_g, w_in, rpb, sink, out_norm_a, out_norm_b, w_out, norm2_g,
           peer_wq, peer_subkeys, peer_u, peer_v, final_g):
    batch, seq, d = x_prompt.shape
    dec_batch, dec_seq, _ = x_sample.shape
    depth = w_mod.shape[0]
    past = cache_na_k.shape[2]
    na_heads = cache_na_k.shape[3]
    swa_kv = cache_swa_k.shape[3]
    swa_heads = sink.shape[1]
    group = swa_heads // swa_kv
    n_ctx = batch * seq
    scale = HEAD_DIM ** -0.5
    qa_col, ka_col, va_col = 0, na_heads, 2 * na_heads
    qb_col = 3 * na_heads
    kb_col = qb_col + swa_heads
    vb_col = kb_col + swa_kv

    x = jnp.concatenate([x_prompt.reshape(n_ctx, d), x_sample.reshape(dec_batch * dec_seq, d)], axis=0)

    n_mod = 1 + dec_batch
    r_mod = -(-n_mod // SUBLANES) * SUBLANES
    cond = jnp.concatenate([c_ctx[None, :], c, jnp.zeros((r_mod - n_mod, d), F32)], axis=0)
    mod = _adaln(cond, w_mod, b_mod).reshape(depth, r_mod, 6, d)

    cos, sin = _rope_tables(dec_seq)
    na_bias = _na_bias(rpb, dec_seq // GRID_W)
    uv_tab = _expert_table(peer_u, peer_v)
    na_k, na_v, swa_k, swa_v = [], [], [], []
    for l in range(depth):
        (h,) = _norm_mod(x, norm1_g, mod, l, 0, 1, n_ctx, dec_seq, [BF16])
        p = _matmul(h, w_in, l, n_ctx, dec_seq)
        oa_ctx = _ctx_attn(p, None, batch=batch, seq=seq, n_kv=na_heads, group=1,
                           q_col=qa_col, k_col=ka_col, v_col=va_col, scale=scale)
        ob_ctx = _ctx_attn(p, sink[l], batch=batch, seq=seq, n_kv=swa_kv, group=group,
                           q_col=qb_col, k_col=kb_col, v_col=vb_col, scale=scale)
        oa_lat = _na_lat(p, cache_na_k, cache_na_v, na_bias, l,
                         n_ctx=n_ctx, batch=dec_batch, t=dec_seq, heads=na_heads, past=past, scale=scale)
        ob_lat = _swa_lat(p, cache_swa_k, cache_swa_v, sink[l], cos, sin, l,
                          n_ctx=n_ctx, batch=dec_batch, t=dec_seq, n_kv=swa_kv, group=group, past=past,
                          q_col=qb_col, k_col=kb_col, v_col=vb_col, scale=scale)
        hn = _group_norm(oa_ctx, ob_ctx, oa_lat, ob_lat, out_norm_a, out_norm_b, l)
        x = _matmul_residual(hn, w_out, x, mod, l, 2, n_ctx, dec_seq)

        h2_bf, h2 = _norm_mod(x, norm2_g, mod, l, 3, 4, n_ctx, dec_seq, [BF16, F32])
        q = _matmul(h2_bf, peer_wq, l, n_ctx, dec_seq)
        idx_t, g_t = _peer_topk(q, peer_subkeys, l)
        x = _peer(idx_t.T, g_t, h2, x, mod, uv_tab, l, 5, n_ctx, dec_seq)

        pc = p[:n_ctx]
        na_k.append(pc[:, ka_col * HEAD_DIM:va_col * HEAD_DIM].reshape(batch, seq, na_heads, HEAD_DIM))
        na_v.append(pc[:, va_col * HEAD_DIM:qb_col * HEAD_DIM].reshape(batch, seq, na_heads, HEAD_DIM))
        swa_k.append(pc[:, kb_col * HEAD_DIM:vb_col * HEAD_DIM].reshape(batch, seq, swa_kv, HEAD_DIM))
        swa_v.append(pc[:, vb_col * HEAD_DIM:].reshape(batch, seq, swa_kv, HEAD_DIM))

    y = _final_norm(x, final_g)
    return (y[:n_ctx].reshape(batch, seq, d), y[n_ctx:].reshape(dec_batch, dec_seq, d),
            jnp.stack(na_k, axis=1), jnp.stack(na_v, axis=1),
            jnp.stack(swa_k, axis=1), jnp.stack(swa_v, axis=1))
```
